```python
import math
import jax, jax.numpy as jnp
from jax import lax
import numpy as np


D_MODEL = 2048
BATCH = 4
SEQ = 2048
DEPTH = 2

PLE_DIM = 256
NORM_EPS = 1e-6
POOL_WIDTH = D_MODEL // 2
POOL_WINDOWS = (2, 4, 8, 16)
POOL_GROUP = POOL_WIDTH // len(POOL_WINDOWS)
HYENA_WIDTH = D_MODEL // 2
HYENA_ORDER = 2
N_DIRS = 2
SHORT_CONV = 3
FILTER_EMB = 33
FILTER_HIDDEN = 64
DECAY_TARGET = 1e-2
DECAY_SHORT_PCT = 0.3
DECAY_LONG_PCT = 1.5
DECAY_MIN = math.log(DECAY_TARGET) / DECAY_LONG_PCT
DECAY_MAX = math.log(DECAY_TARGET) / DECAY_SHORT_PCT
IN_COLS = POOL_WIDTH + (HYENA_ORDER + 1) * HYENA_WIDTH + 2 * D_MODEL
PEER_HEADS = 8
PEER_NKEYS = 128
PEER_N = PEER_NKEYS * PEER_NKEYS
PEER_QDIM = 256
PEER_HALF = PEER_QDIM // 2
PEER_TOPK = 16
PEER_CHUNK = 128

kernel_name = 'hybrid_pool_hyena_peer_encoder'


def _rms_norm(x, g):
    xf = x.astype(jnp.float32)
    y = xf * lax.rsqrt(jnp.mean(xf * xf, axis=-1, keepdims=True) + NORM_EPS)
    return (y * g.astype(jnp.float32)).astype(x.dtype)


def _short_conv(u, w, b):
    L = u.shape[1]
    pad = SHORT_CONV // 2
    up = jnp.pad(u, ((0, 0), (pad, SHORT_CONV - 1 - pad), (0, 0)))
    return sum(up[:, k:k + L] * w[k] for k in range(SHORT_CONV)) + b


def _multiscale_pool(u, pool_w, pool_scale):
    B, L, C = u.shape
    csum = jnp.cumsum(u.astype(jnp.float32), axis=1)
    csum = jnp.concatenate([jnp.zeros((B, 1, C), jnp.float32), csum], axis=1)
    pos = jnp.arange(L)
    outs = []
    for gi, win in enumerate(POOL_WINDOWS):
        lo = jnp.clip(pos - win // 2, 0, L - 1)
        hi = jnp.clip(pos + win - win // 2 - 1, 0, L - 1)
        cg = csum[..., gi * POOL_GROUP:(gi + 1) * POOL_GROUP]
        ug = u[..., gi * POOL_GROUP:(gi + 1) * POOL_GROUP].astype(jnp.float32)
        count = (hi - lo + 1).astype(jnp.float32)[None, :, None]
        mean = (cg[:, hi + 1] - cg[:, lo]) / count
        outs.append((mean - ug).astype(u.dtype) @ pool_w[gi])
    return jnp.concatenate(outs, axis=-1) * pool_scale


def _hyena_filters(L, w1, b1, w2, b2, w3, freq):
    f32 = jnp.float32
    t = jnp.linspace(0.0, 1.0, L, dtype=f32)[:, None]
    bands = (FILTER_EMB - 1) // 2
    fb = jnp.linspace(1e-4, bands - 1, bands, dtype=f32)[None, :]
    ang = (2.0 * math.pi / L) * jnp.arange(L, dtype=f32)[:, None] * fb
    z = jnp.concatenate([t, jnp.cos(ang), -jnp.sin(ang)], axis=-1)
    fr = freq.astype(f32)
    hdn = jnp.sin(fr * (z @ w1.astype(f32) + b1.astype(f32)))
    hdn = jnp.sin(fr * (hdn @ w2.astype(f32) + b2.astype(f32)))
    filt = (hdn @ w3.astype(f32)).reshape(L, N_DIRS, HYENA_ORDER, HYENA_WIDTH)
    deltas = jnp.abs(jnp.linspace(DECAY_MIN, DECAY_MAX, HYENA_WIDTH, dtype=f32))
    filt = filt * jnp.exp(-t.reshape(L, 1, 1, 1) * deltas)
    return filt / (jnp.sum(jnp.abs(filt), axis=0, keepdims=True) + 1e-6)


def _bidir_long_conv(u, h_dirs, bias):
    L = u.shape[1]
    n = 2 * L
    uf = u.astype(jnp.float32)
    U = jnp.fft.rfft(uf, n=n, axis=1)
    H = jnp.fft.rfft(h_dirs, n=n, axis=0)
    Hc = H[:, 0] + jnp.conj(H[:, 1])
    y = jnp.fft.irfft(U * Hc[None], n=n, axis=1)[:, :L]
    return (y + uf * bias.astype(jnp.float32)).astype(u.dtype)


def _hyena(proj, conv_w, conv_b, filt, filt_bias):
    uc = _short_conv(proj, conv_w, conv_b)
    v, x1, x2 = jnp.split(uc, HYENA_ORDER + 1, axis=-1)
    z = _bidir_long_conv(v, filt[:, :, 0], filt_bias[0]) * x1
    z = _bidir_long_conv(z, filt[:, :, 1], filt_bias[1]) * x2
    return z


def _peer(xn, wq, k1, k2, u_tab, v_tab):
    B, S, D = xn.shape
    T = B * S
    xt = xn.reshape(T, D)
    q = (xt @ wq).reshape(T, PEER_HEADS, PEER_QDIM).astype(jnp.float32)
    s1 = jnp.einsum('thd,kd->thk', q[..., :PEER_HALF], k1.astype(jnp.float32))
    s2 = jnp.einsum('thd,kd->thk', q[..., PEER_HALF:], k2.astype(jnp.float32))
    v1, i1 = lax.top_k(s1, PEER_TOPK)
    v2, i2 = lax.top_k(s2, PEER_TOPK)
    cand = (v1[..., :, None] + v2[..., None, :]).reshape(T, PEER_HEADS, PEER_TOPK * PEER_TOPK)
    cand_idx = (i1[..., :, None] * PEER_NKEYS + i2[..., None, :]).reshape(T, PEER_HEADS, PEER_TOPK * PEER_TOPK)
    best, sel = lax.top_k(cand, PEER_TOPK)
    idx = jnp.take_along_axis(cand_idx, sel, axis=-1).reshape(T, PEER_HEADS * PEER_TOPK)
    gate = jax.nn.softmax(best, axis=-1).reshape(T, PEER_HEADS * PEER_TOPK).astype(xn.dtype)
    nc = T // PEER_CHUNK

    def chunk_fn(args):
        xc, ic, gc = args
        hid = jnp.einsum('cd,ced->ce', xc, u_tab[ic])
        act = gc * jax.nn.gelu(hid, approximate=False)
        return jnp.einsum('ce,ced->cd', act, v_tab[ic])

    out = lax.map(chunk_fn, (xt.reshape(nc, PEER_CHUNK, D),
                             idx.reshape(nc, PEER_CHUNK, -1),
                             gate.reshape(nc, PEER_CHUNK, -1)))
    return out.reshape(B, S, D)


def setup_inputs(seed: int = 0) -> dict:
    key = jax.random.key(seed)
    ks = jax.random.split(key, 32)
    f32 = jnp.float32

    def nrm(k, shape, scale):
        return jax.random.normal(k, shape, f32) * scale

    def gain(k, shape):
        return 1.0 + 0.02 * jax.random.normal(k, shape, f32)

    return {
        'x': nrm(ks[0], (BATCH, SEQ, D_MODEL), 1.0),
        'p': nrm(ks[1], (DEPTH, BATCH, SEQ, PLE_DIM), 1.0),
        'norm_mix_g': gain(ks[2], (DEPTH, D_MODEL)),
        'w_in': nrm(ks[3], (DEPTH, D_MODEL, IN_COLS), D_MODEL ** -0.5),
        'pool_w': nrm(ks[4], (DEPTH, len(POOL_WINDOWS), POOL_GROUP, POOL_GROUP), POOL_GROUP ** -0.5),
        'pool_scale': gain(ks[5], (DEPTH, POOL_WIDTH)),
        'w_pool_out': nrm(ks[6], (DEPTH, POOL_WIDTH, D_MODEL), POOL_WIDTH ** -0.5),
        'short_conv_w': nrm(ks[7], (DEPTH, SHORT_CONV, (HYENA_ORDER + 1) * HYENA_WIDTH), SHORT_CONV ** -0.5),
        'short_conv_b': nrm(ks[8], (DEPTH, (HYENA_ORDER + 1) * HYENA_WIDTH), 0.02),
        'filt_w1': nrm(ks[9], (DEPTH, FILTER_EMB, FILTER_HIDDEN), FILTER_EMB ** -0.5),
        'filt_b1': nrm(ks[10], (DEPTH, FILTER_HIDDEN), 0.1),
        'filt_w2': nrm(ks[11], (DEPTH, FILTER_HIDDEN, FILTER_HIDDEN), FILTER_HIDDEN ** -0.5),
        'filt_b2': nrm(ks[12], (DEPTH, FILTER_HIDDEN), 0.1),
        'filt_w3': nrm(ks[13], (DEPTH, FILTER_HIDDEN, N_DIRS * HYENA_ORDER * HYENA_WIDTH), FILTER_HIDDEN ** -0.5),
        'filt_freq': gain(ks[14], (DEPTH, FILTER_HIDDEN)),
        'filt_bias': nrm(ks[15], (DEPTH, HYENA_ORDER, HYENA_WIDTH), 0.1),
        'w_hyena_out': nrm(ks[16], (DEPTH, HYENA_WIDTH, D_MODEL), HYENA_WIDTH ** -0.5),
        'w_o': nrm(ks[17], (DEPTH, D_MODEL, D_MODEL), D_MODEL ** -0.5),
        'norm_ffn_g': gain(ks[18], (DEPTH, D_MODEL)),
        'peer_wq': nrm(ks[19], (DEPTH, D_MODEL, PEER_HEADS * PEER_QDIM), D_MODEL ** -0.5),
        'peer_k1': nrm(ks[20], (DEPTH, PEER_NKEYS, PEER_HALF), PEER_HALF ** -0.5),
        'peer_k2': nrm(ks[21], (DEPTH, PEER_NKEYS, PEER_HALF), PEER_HALF ** -0.5),
        'peer_u': nrm(ks[22], (DEPTH, PEER_N, D_MODEL), D_MODEL ** -0.5),
        'peer_v': nrm(ks[23], (DEPTH, PEER_N, D_MODEL), PEER_HEADS ** -0.5),
        'norm_ple_g': gain(ks[24], (DEPTH, D_MODEL)),
        'ple_w_gate': nrm(ks[25], (DEPTH, D_MODEL, D_MODEL), D_MODEL ** -0.5),
        'ple_w_proj': nrm(ks[26], (DEPTH, PLE_DIM, D_MODEL), PLE_DIM ** -0.5),
        'final_norm_g': gain(ks[27], (D_MODEL,)),
    }


def reference(x, p, norm_mix_g, w_in, pool_w, pool_scale, w_pool_out, short_conv_w, short_conv_b,
              filt_w1, filt_b1, filt_w2, filt_b2, filt_w3, filt_freq, filt_bias, w_hyena_out, w_o,
              norm_ffn_g, peer_wq, peer_k1, peer_k2, peer_u, peer_v, norm_ple_g, ple_w_gate,
              ple_w_proj, final_norm_g):
    h = x
    L = x.shape[1]
    split_hy = POOL_WIDTH
    split_gate = POOL_WIDTH + (HYENA_ORDER + 1) * HYENA_WIDTH
    for i in range(DEPTH):
        xn = _rms_norm(h, norm_mix_g[i])
        proj = xn @ w_in[i]
        pool_in = proj[..., :split_hy]
        hy_in = proj[..., split_hy:split_gate]
        g_pool, g_hy = jnp.split(jax.nn.sigmoid(proj[..., split_gate:]), 2, axis=-1)
        a = _multiscale_pool(pool_in, pool_w[i], pool_scale[i]) @ w_pool_out[i]
        filt = _hyena_filters(L, filt_w1[i], filt_b1[i], filt_w2[i], filt_b2[i], filt_w3[i], filt_freq[i])
        b = _hyena(hy_in, short_conv_w[i], short_conv_b[i], filt, filt_bias[i]) @ w_hyena_out[i]
        h = h + (g_pool * a + g_hy * b) @ w_o[i]
        hn = _rms_norm(h, norm_ffn_g[i])
        h = h + _peer(hn, peer_wq[i], peer_k1[i], peer_k2[i], peer_u[i], peer_v[i])
        hn = _rms_norm(h, norm_ple_g[i])
        h = h + jax.nn.sigmoid(hn @ ple_w_gate[i]) * (p[i] @ ple_w_proj[i])
    return _rms_norm(h, final_norm_g)
```

```python
import functools
import math

import jax
import jax.numpy as jnp
from jax import lax
from jax.experimental import pallas as pl
from jax.experimental.pallas import tpu as pltpu

F32 = jnp.float32
BF16 = jnp.bfloat16

NORM_EPS = 1e-6
POOL_WINDOWS = (2, 4, 8, 16)
SHORT_CONV = 3
FILTER_EMB = 33
DECAY_TARGET = 1e-2
DECAY_MIN = math.log(DECAY_TARGET) / 1.5
DECAY_MAX = math.log(DECAY_TARGET) / 0.3
PEER_TOPK = 16
LANES = 128
V7X_VMEM_BYTES = 64 * 1024 * 1024
VMEM_LIMIT = V7X_VMEM_BYTES - 8 * 1024 * 1024
HI = lax.Precision.HIGHEST


def _tile(n, pref):
    return pref if n % pref == 0 else n


def _params(sem):
    return pltpu.CompilerParams(dimension_semantics=sem, vmem_limit_bytes=VMEM_LIMIT)


def _bdot(a, b):
    return jnp.dot(a.astype(BF16), b.astype(BF16), preferred_element_type=F32)


def _rms(x, g):
    ms = jnp.mean(x * x, axis=-1, keepdims=True)
    return x * lax.rsqrt(ms + NORM_EPS) * g


def _inproj_kernel(x_ref, g_ref, w_ref, o_ref, xn_ref, *, n_plain):
    j = pl.program_id(1)

    @pl.when(j == 0)
    def _():
        xn_ref[...] = _rms(x_ref[...], g_ref[...]).astype(BF16)

    acc = jnp.dot(xn_ref[...], w_ref[...].astype(BF16), preferred_element_type=F32)

    @pl.when(j < n_plain)
    def _():
        o_ref[...] = acc.astype(o_ref.dtype)

    @pl.when(j >= n_plain)
    def _():
        o_ref[...] = jax.nn.sigmoid(acc).astype(o_ref.dtype)


def _inproj(h, g, w, n_gate_cols):
    T, D = h.shape
    N = w.shape[1]
    tm, tn = _tile(T, 1024), _tile(N, 512)
    assert (N - n_gate_cols) % tn == 0
    return pl.pallas_call(
        functools.partial(_inproj_kernel, n_plain=(N - n_gate_cols) // tn),
        grid=(T // tm, N // tn),
        in_specs=[pl.BlockSpec((tm, D), lambda i, j: (i, 0)),
                  pl.BlockSpec((1, D), lambda i, j: (0, 0)),
                  pl.BlockSpec((D, tn), lambda i, j: (0, j))],
        out_specs=pl.BlockSpec((tm, tn), lambda i, j: (i, j)),
        out_shape=jax.ShapeDtypeStruct((T, N), BF16),
        scratch_shapes=[pltpu.VMEM((tm, D), BF16)],
        compiler_params=_params(("parallel", "arbitrary")),
        name="inproj",
    )(h, g.reshape(1, D), w)


def _pool_kernel(pin_ref, gate_ref, pw_ref, ps_ref, wo_ref, o_ref, *, tm, kw, seq):
    t0 = pl.program_id(1) * tm
    k0 = pl.multiple_of(jnp.clip(t0 - (kw - tm) // 2, 0, seq - kw), LANES)
    r = t0 + lax.broadcasted_iota(jnp.int32, (tm, 1), 0)
    s = k0 + lax.broadcasted_iota(jnp.int32, (1, kw), 1)
    group = pw_ref.shape[1]
    parts = []
    for gi, win in enumerate(POOL_WINDOWS):
        lo = jnp.clip(r - win // 2, 0, seq - 1)
        hi = jnp.clip(r + win - win // 2 - 1, 0, seq - 1)
        inv = 1.0 / (hi - lo + 1).astype(F32)
        band = jnp.where((s >= lo) & (s <= hi), inv, 0.0) - jnp.where(s == r, 1.0, 0.0)
        u = pin_ref[0, pl.ds(k0, kw), gi * group:(gi + 1) * group]
        centred = jnp.dot(band.astype(BF16), u, preferred_element_type=F32)
        parts.append(_bdot(centred, pw_ref[gi]))
    pooled = jnp.concatenate(parts, axis=-1) * ps_ref[...]
    a = _bdot(pooled, wo_ref[...])
    o_ref[0] = (a * gate_ref[0].astype(F32)).astype(o_ref.dtype)


def _pool_branch(proj, pool_w, pool_scale, w_pool_out, gate_col):
    B, L, _ = proj.shape
    pw, D = w_pool_out.shape
    tm = _tile(L, 256)
    kw = min(L, 2 * tm)
    assert max(POOL_WINDOWS) <= (kw - tm) // 2 or kw == L
    return pl.pallas_call(
        functools.partial(_pool_kernel, tm=tm, kw=kw, seq=L),
        grid=(B, L // tm),
        in_specs=[pl.BlockSpec((1, L, pw), lambda b, i: (b, 0, 0)),
                  pl.BlockSpec((1, tm, D), lambda b, i: (b, i, gate_col // D)),
                  pl.BlockSpec(pool_w.shape, lambda b, i: (0, 0, 0)),
                  pl.BlockSpec((1, pw), lambda b, i: (0, 0)),
                  pl.BlockSpec((pw, D), lambda b, i: (0, 0))],
        out_specs=pl.BlockSpec((1, tm, D), lambda b, i: (b, i, 0)),
        out_shape=jax.ShapeDtypeStruct((B, L, D), BF16),
        compiler_params=_params(("parallel", "arbitrary")),
        name="pool_branch",
    )(proj, proj, pool_w.astype(BF16), pool_scale.reshape(1, pw), w_pool_out.astype(BF16))


def _filter_kernel(fb_ref, w1t_ref, w1c_ref, w1s_ref, b1_ref, w2_ref, b2_ref, fr_ref, dl_ref,
                   w3a_ref, w3b_ref, w3c_ref, w3d_ref, hs_ref, hd_ref, nq_ref, hdn_ref, *, seq):
    @pl.when(pl.program_id(0) == 0)
    def _():
        ti = lax.broadcasted_iota(jnp.int32, (seq, 1), 0).astype(F32)
        t = ti / (seq - 1.0)
        ang = (2.0 * math.pi / seq) * ti * fb_ref[...]
        z = (t * w1t_ref[...] + jnp.dot(jnp.cos(ang), w1c_ref[...], precision=HI, preferred_element_type=F32)
             - jnp.dot(jnp.sin(ang), w1s_ref[...], precision=HI, preferred_element_type=F32))
        hdn = jnp.sin(fr_ref[...] * (z + b1_ref[...]))
        hdn = jnp.sin(fr_ref[...] * (jnp.dot(hdn, w2_ref[...], precision=HI, preferred_element_type=F32)
                                    + b2_ref[...]))
        hdn_ref[...] = hdn

    ti = lax.broadcasted_iota(jnp.int32, (seq, 1), 0)
    t = ti.astype(F32) / (seq - 1.0)
    decay = jnp.exp(-t * dl_ref[...])
    sign = jnp.where(ti % 2 == 0, 1.0, -1.0)

    def filt(w3_ref):
        f = jnp.dot(hdn_ref[...], w3_ref[...], precision=HI, preferred_element_type=F32) * decay
        return f / (jnp.sum(jnp.abs(f), axis=0, keepdims=True) + 1e-6)

    for order, (wf_ref, wb_ref) in enumerate(((w3a_ref, w3c_ref), (w3b_ref, w3d_ref))):
        hf, hb = filt(wf_ref), filt(wb_ref)
        hsum = hf + hb
        hs_ref[order] = hsum.astype(hs_ref.dtype)
        hd_ref[order] = (hf - hb).astype(hd_ref.dtype)
        nq_ref[order] = jnp.sum(hsum * sign, axis=0, keepdims=True)


def _pad_to(a, shape):
    return jnp.pad(a, [(0, s - d) for d, s in zip(a.shape, shape)])


def _hyena_filters(seq, w1, b1, w2, b2, w3, freq, width):
    hid = w1.shape[1]
    bands = (FILTER_EMB - 1) // 2
    hp = max(LANES, hid)
    fb = _pad_to(jnp.linspace(1e-4, bands - 1, bands, dtype=F32)[None, :], (1, LANES))
    w1t = _pad_to(w1[0:1], (1, hp))
    w1c = _pad_to(w1[1:1 + bands], (LANES, hp))
    w1s = _pad_to(w1[1 + bands:], (LANES, hp))
    b1p, b2p, frp = (_pad_to(v.reshape(1, hid), (1, hp)) for v in (b1, b2, freq))
    w2p = _pad_to(w2, (hp, hp))
    w3p = _pad_to(w3, (hp, w3.shape[1]))
    deltas = jnp.abs(jnp.linspace(DECAY_MIN, DECAY_MAX, width, dtype=F32))[None, :]
    tc = _tile(width, 256)
    nc = width // tc
    small = lambda shape: pl.BlockSpec(shape, lambda c: (0,) * len(shape))
    w3spec = lambda g: pl.BlockSpec((hp, tc), lambda c, g=g: (0, g * nc + c))
    out3 = lambda rows: pl.BlockSpec((2, rows, tc), lambda c: (0, 0, c))
    return pl.pallas_call(
        functools.partial(_filter_kernel, seq=seq),
        grid=(nc,),
        in_specs=[small((1, LANES)), small((1, hp)), small((LANES, hp)), small((LANES, hp)), small((1, hp)),
                  small((hp, hp)), small((1, hp)), small((1, hp)), pl.BlockSpec((1, tc), lambda c: (0, c)),
                  w3spec(0), w3spec(1), w3spec(2), w3spec(3)],
        out_specs=[out3(seq), out3(seq), out3(1)],
        out_shape=[jax.ShapeDtypeStruct((2, seq, width), BF16), jax.ShapeDtypeStruct((2, seq, width), BF16),
                   jax.ShapeDtypeStruct((2, 1, width), F32)],
        scratch_shapes=[pltpu.VMEM((seq, hp), F32)],
        compiler_params=_params(("arbitrary",)),
        name="hyena_filters",
    )(fb, w1t, w1c, w1s, b1p, w2p, b2p, frp, deltas, w3p, w3p, w3p, w3p)


def _dft_tables(seq, fc):
    n = 2 * seq
    f = jnp.arange(seq, dtype=jnp.int32)[:, None]
    t = jnp.arange(seq, dtype=jnp.int32)[None, :]
    ang = ((f * t) & (n - 1)).astype(F32) * (2.0 * math.pi / n)
    cosm = jnp.cos(ang)
    sinm = jnp.where(f == 0, jnp.where(t % 2 == 0, 1.0, -1.0), -jnp.sin(ang))
    nk = seq // fc
    fwd = jnp.concatenate([cosm.reshape(nk, fc, seq), sinm.reshape(nk, fc, seq)], axis=1)
    inv = jnp.transpose(fwd, (0, 2, 1)) * (2.0 / n)
    return fwd.astype(BF16), inv.astype(BF16)


def _spectrum_kernel(w_ref, hs_ref, hd_ref, gr_ref, gi_ref, *, fc):
    gr_ref[0] = jnp.dot(w_ref[0, :fc, :], hs_ref[0], preferred_element_type=F32)
    gi_ref[0] = jnp.dot(w_ref[0, fc:, :], hd_ref[0], preferred_element_type=F32)


def _filter_spectrum(fwd, hs, hd):
    nk, fc2, seq = fwd.shape
    fc = fc2 // 2
    order, _, width = hs.shape
    tc = _tile(width, 512)
    hspec = pl.BlockSpec((1, seq, tc), lambda k, o, c: (o, 0, c))
    gspec = pl.BlockSpec((1, fc, tc), lambda k, o, c: (o, k, c))
    gshape = jax.ShapeDtypeStruct((order, seq, width), F32)
    return pl.pallas_call(
        functools.partial(_spectrum_kernel, fc=fc),
        grid=(nk, order, width // tc),
        in_specs=[pl.BlockSpec((1, fc2, seq), lambda k, o, c: (k, 0, 0)), hspec, hspec],
        out_specs=[gspec, gspec],
        out_shape=[gshape, gshape],
        compiler_params=_params(("arbitrary", "arbitrary", "arbitrary")),
        name="filter_spectrum",
    )(fwd, hs, hd)


def _hyena_kernel(v_ref, x1_ref, x2_ref, cwv_ref, cw1_ref, cw2_ref, cbv_ref, cb1_ref, cb2_ref,
                  fwd_ref, inv_ref, gr_ref, gi_ref, nq_ref, fbias_ref, o_ref,
                  u_ref, ub_ref, x1c_ref, x2c_ref, acc_ref, *, seq, fc):
    o, k = pl.program_id(2), pl.program_id(3)
    last = pl.num_programs(3) - 1
    row = lax.broadcasted_iota(jnp.int32, (seq, 1), 0)

    def short_conv(x_ref, w_ref, b_ref):
        x = x_ref[0].astype(F32)
        prev = jnp.where(row == 0, 0.0, pltpu.roll(x, 1, 0))
        nxt = jnp.where(row == seq - 1, 0.0, pltpu.roll(x, seq - 1, 0))
        return prev * w_ref[0:1, :] + x * w_ref[1:2, :] + nxt * w_ref[2:3, :] + b_ref[...]

    @pl.when((o == 0) & (k == 0))
    def _():
        v = short_conv(v_ref, cwv_ref, cbv_ref)
        u_ref[...] = v
        ub_ref[...] = v.astype(BF16)
        x1c_ref[...] = short_conv(x1_ref, cw1_ref, cb1_ref)
        x2c_ref[...] = short_conv(x2_ref, cw2_ref, cb2_ref)

    @pl.when(k == 0)
    def _():
        acc_ref[...] = jnp.zeros_like(acc_ref)

    spec = jnp.dot(fwd_ref[0], ub_ref[...], preferred_element_type=F32)
    vr, vi = spec[:fc], spec[fc:]
    gr, gi = gr_ref[0], gi_ref[0]
    special = (lax.broadcasted_iota(jnp.int32, (fc, 1), 0) == 0) & (k == 0)
    yr = jnp.where(special, 0.5 * vr * gr, vr * gr - vi * gi)
    yi = jnp.where(special, 0.5 * vi * nq_ref[0], vr * gi + vi * gr)
    y = jnp.concatenate([yr, yi], axis=0).astype(BF16)
    acc_ref[...] += jnp.dot(inv_ref[0], y, preferred_element_type=F32)

    @pl.when((k == last) & (o == 0))
    def _():
        z = (acc_ref[...] + u_ref[...] * fbias_ref[0]) * x1c_ref[...]
        u_ref[...] = z
        ub_ref[...] = z.astype(BF16)

    @pl.when((k == last) & (o == 1))
    def _():
        o_ref[0] = ((acc_ref[...] + u_ref[...] * fbias_ref[0]) * x2c_ref[...]).astype(o_ref.dtype)


def _hyena_operator(proj, conv_w, conv_b, fwd, inv, gr, gi, nq, filt_bias, col0, width):
    B, L, _ = proj.shape
    nk, fc2, _ = fwd.shape
    tc = _tile(width, 256)
    nc = width // tc
    assert col0 % tc == 0
    xspec = lambda part: pl.BlockSpec((1, L, tc), lambda b, c, o, k, part=part: (b, 0, col0 // tc + part * nc + c))
    wspec = lambda rows, part: pl.BlockSpec((rows, tc), lambda b, c, o, k, part=part: (0, part * nc + c))
    ospec = lambda rows: pl.BlockSpec((1, rows, tc), lambda b, c, o, k: (o, 0, c))
    cb = conv_b.reshape(1, -1)
    return pl.pallas_call(
        functools.partial(_hyena_kernel, seq=L, fc=fc2 // 2),
        grid=(B, nc, 2, nk),
        in_specs=[xspec(0), xspec(1), xspec(2), wspec(SHORT_CONV, 0), wspec(SHORT_CONV, 1), wspec(SHORT_CONV, 2),
                  wspec(1, 0), wspec(1, 1), wspec(1, 2),
                  pl.BlockSpec((1, fc2, L), lambda b, c, o, k: (k, 0, 0)),
                  pl.BlockSpec((1, L, fc2), lambda b, c, o, k: (k, 0, 0)),
                  pl.BlockSpec((1, fc2 // 2, tc), lambda b, c, o, k: (o, k, c)),
                  pl.BlockSpec((1, fc2 // 2, tc), lambda b, c, o, k: (o, k, c)),
                  ospec(1), ospec(1)],
        out_specs=pl.BlockSpec((1, L, tc), lambda b, c, o, k: (b, 0, c)),
        out_shape=jax.ShapeDtypeStruct((B, L, width), BF16),
        scratch_shapes=[pltpu.VMEM((L, tc), F32), pltpu.VMEM((L, tc), BF16), pltpu.VMEM((L, tc), F32),
                        pltpu.VMEM((L, tc), F32), pltpu.VMEM((L, tc), F32)],
        compiler_params=_params(("arbitrary",) * 4),
        name="hyena_operator",
    )(proj, proj, proj, conv_w, conv_w, conv_w, cb, cb, cb, fwd, inv, gr, gi, nq,
      filt_bias.reshape(2, 1, width))


def _merge_kernel(ga_ref, z_ref, gate_ref, wh_ref, wo_ref, h_ref, g_ref, h1_ref, hn_ref, hnt_ref):
    hy = jnp.dot(z_ref[...], wh_ref[...], preferred_element_type=F32)
    mix = ga_ref[...].astype(F32) + gate_ref[...].astype(F32) * hy
    h1 = h_ref[...] + _bdot(mix, wo_ref[...])
    h1_ref[...] = h1
    hn = _rms(h1, g_ref[...])
    hn_ref[...] = hn.astype(BF16)
    hnt_ref[...] = hn.T.astype(BF16)


def _merge(ga, z2, proj, gate_col, w_hy_out, w_o, h, g):
    T, D = h.shape
    C = z2.shape[1]
    tm = _tile(T, 256)
    row = lambda w: pl.BlockSpec((tm, w), lambda i: (i, 0))
    full = lambda a: pl.BlockSpec(a.shape, lambda i: (0,) * a.ndim)
    whb, wob, g2 = w_hy_out.astype(BF16), w_o.astype(BF16), g.reshape(1, D)
    return pl.pallas_call(
        _merge_kernel,
        grid=(T // tm,),
        in_specs=[row(D), row(C), pl.BlockSpec((tm, D), lambda i: (i, gate_col // D)), full(whb), full(wob),
                  row(D), full(g2)],
        out_specs=[row(D), row(D), pl.BlockSpec((D, tm), lambda i: (0, i))],
        out_shape=[jax.ShapeDtypeStruct((T, D), F32), jax.ShapeDtypeStruct((T, D), BF16),
                   jax.ShapeDtypeStruct((D, T), BF16)],
        compiler_params=_params(("parallel",)),
        name="mixer_merge",
    )(ga, z2, proj, whb, wob, h, g2)


def _extract_top(x, count):
    out = []
    for _ in range(count):
        m = jnp.max(x, axis=0, keepdims=True)
        out.append(m)
        x = jnp.where(x == m, -jnp.inf, x)
    return out


def _route_kernel(hn_ref, wq_ref, k1_ref, k2_ref, s2_ref, e2_ref, d_ref, e1_ref, q_ref, *, heads, half):
    q_ref[...] = jnp.dot(hn_ref[...], wq_ref[...], preferred_element_type=F32)
    k1, k2 = k1_ref[...].astype(BF16), k2_ref[...].astype(BF16)
    nt = (((1,), (1,)), ((), ()))
    nbest = PEER_TOPK + 1
    span = [-(-(nbest // (a + 1)) // 8) * 8 for a in range(nbest)]

    def body(h, carry):
        base = pl.multiple_of(h * 2 * half, 2 * half)
        q1 = q_ref[:, pl.ds(base, half)].astype(BF16)
        q2 = q_ref[:, pl.ds(base + half, half)].astype(BF16)
        s1 = lax.dot_general(k1, q1, nt, preferred_element_type=F32)
        s2 = lax.dot_general(k2, q2, nt, preferred_element_type=F32)
        v1 = _extract_top(s1, nbest)
        v2 = _extract_top(s2, nbest)
        v2 = jnp.concatenate(v2 + [jnp.full_like(v2[0], -jnp.inf)] * (span[0] - nbest), axis=0)
        cand = jnp.concatenate([v1[a] + v2[:span[a]] for a in range(nbest)], axis=0)
        best = _extract_top(cand, nbest)
        top = best[0]
        z = sum(jnp.exp(b - top) for b in best[1:PEER_TOPK]) + 1.0
        thr = 0.5 * (best[PEER_TOPK - 1] + best[PEER_TOPK])
        s2_ref[h] = s2
        e2_ref[h] = jnp.exp(s2 - v2[0:1]) / z
        d_ref[h] = thr - s1
        e1_ref[h] = jnp.exp(s1 - v1[0])
        return carry

    lax.fori_loop(0, heads, body, 0)


def _peer_route(hn, wq, k1, k2):
    T, D = hn.shape
    nk, half = k1.shape
    heads = wq.shape[1] // (2 * half)
    tm = _tile(T, 256)
    wqb = wq.astype(BF16)
    ospec = pl.BlockSpec((heads, nk, tm), lambda i: (0, 0, i))
    oshape = jax.ShapeDtypeStruct((heads, nk, T), F32)
    return pl.pallas_call(
        functools.partial(_route_kernel, heads=heads, half=half),
        grid=(T // tm,),
        in_specs=[pl.BlockSpec((tm, D), lambda i: (i, 0)), pl.BlockSpec(wqb.shape, lambda i: (0, 0)),
                  pl.BlockSpec(k1.shape, lambda i: (0, 0)), pl.BlockSpec(k2.shape, lambda i: (0, 0))],
        out_specs=[ospec] * 4,
        out_shape=[oshape] * 4,
        scratch_shapes=[pltpu.VMEM((tm, wq.shape[1]), F32)],
        compiler_params=_params(("parallel",)),
        name="peer_route",
    )(hn, wqb, k1, k2)


def _experts_kernel(hnt_ref, u_ref, v_ref, s2_ref, e2_ref, d_ref, e1_ref, h1_ref, o_ref, at_ref,
                    *, heads, nk, tm, te):
    j = pl.program_id(1)

    @pl.when(j == 0)
    def _():
        o_ref[...] = h1_ref[...]

    hid = jnp.dot(u_ref[...], hnt_ref[...], preferred_element_type=F32)
    at_ref[...] = 0.5 * hid * (1.0 + lax.erf(hid * (1.0 / math.sqrt(2.0))))

    groups = te // nk
    g0 = pl.multiple_of(j * groups, groups)

    def body(tb, carry):
        lanes = pl.ds(pl.multiple_of(tb * LANES, LANES), LANES)
        for a in range(groups):
            w = jnp.zeros((nk, LANES), F32)
            for h in range(heads):
                sel = s2_ref[h, :, lanes] >= d_ref[h, pl.ds(g0, groups), lanes][a:a + 1]
                w = w + jnp.where(sel, e2_ref[h, :, lanes], 0.0) * e1_ref[h, pl.ds(g0, groups), lanes][a:a + 1]
            rows = slice(a * nk, (a + 1) * nk)
            at_ref[rows, lanes] = at_ref[rows, lanes] * w
        return carry

    lax.fori_loop(0, tm // LANES, body, 0)
    o_ref[...] += jnp.dot(at_ref[...].T.astype(BF16), v_ref[...], preferred_element_type=F32)


def _peer_experts(hnt, u_tab, v_tab, s2, e2, d, e1, h1):
    D, T = hnt.shape
    E = u_tab.shape[0]
    heads, nk, _ = s2.shape
    tm, te = _tile(T, 512), _tile(E, 8 * nk)
    assert (te // nk) % 8 == 0
    once = dict(pipeline_mode=pl.Buffered(1))
    rspec = pl.BlockSpec((heads, nk, tm), lambda i, j: (0, 0, i), **once)
    return pl.pallas_call(
        functools.partial(_experts_kernel, heads=heads, nk=nk, tm=tm, te=te),
        grid=(T // tm, E // te),
        in_specs=[pl.BlockSpec((D, tm), lambda i, j: (0, i), **once),
                  pl.BlockSpec((te, D), lambda i, j: (j, 0)), pl.BlockSpec((te, D), lambda i, j: (j, 0)),
                  rspec, rspec, rspec, rspec, pl.BlockSpec((tm, D), lambda i, j: (i, 0), **once)],
        out_specs=pl.BlockSpec((tm, D), lambda i, j: (i, 0)),
        out_shape=jax.ShapeDtypeStruct((T, D), F32),
        scratch_shapes=[pltpu.VMEM((te, tm), F32)],
        compiler_params=_params(("parallel", "arbitrary")),
        name="peer_experts",
    )(hnt, u_tab.astype(BF16), v_tab.astype(BF16), s2, e2, d, e1, h1)


def _ple_kernel(h_ref, g_ref, wg_ref, p_ref, wp_ref, hc_ref, o_ref, xn_ref):
    @pl.when(pl.program_id(1) == 0)
    def _():
        xn_ref[...] = _rms(h_ref[...], g_ref[...]).astype(BF16)

    gate = jax.nn.sigmoid(jnp.dot(xn_ref[...], wg_ref[...].astype(BF16), preferred_element_type=F32))
    o_ref[...] = hc_ref[...] + gate * _bdot(p_ref[...], wp_ref[...])


def _ple(h, g, w_gate, p, w_proj):
    T, D = h.shape
    pd = p.shape[1]
    tm, tn = _tile(T, 512), _tile(D, 512)
    return pl.pallas_call(
        _ple_kernel,
        grid=(T // tm, D // tn),
        in_specs=[pl.BlockSpec((tm, D), lambda i, j: (i, 0)), pl.BlockSpec((1, D), lambda i, j: (0, 0)),
                  pl.BlockSpec((D, tn), lambda i, j: (0, j)), pl.BlockSpec((tm, pd), lambda i, j: (i, 0)),
                  pl.BlockSpec((pd, tn), lambda i, j: (0, j)), pl.BlockSpec((tm, tn), lambda i, j: (i, j))],
        out_specs=pl.BlockSpec((tm, tn), lambda i, j: (i, j)),
        out_shape=jax.ShapeDtypeStruct((T, D), F32),
        scratch_shapes=[pltpu.VMEM((tm, D), BF16)],
        compiler_params=_params(("parallel", "arbitrary")),
        name="ple",
    )(h, g.reshape(1, D), w_gate, p, w_proj, h)


def _final_norm_kernel(h_ref, g_ref, o_ref):
    o_ref[...] = _rms(h_ref[...], g_ref[...])


def _final_norm(h, g):
    T, D = h.shape
    tm = _tile(T, 512)
    return pl.pallas_call(
        _final_norm_kernel,
        grid=(T // tm,),
        in_specs=[pl.BlockSpec((tm, D), lambda i: (i, 0)), pl.BlockSpec((1, D), lambda i: (0, 0))],
        out_specs=pl.BlockSpec((tm, D), lambda i: (i, 0)),
        out_shape=jax.ShapeDtypeStruct((T, D), F32),
        compiler_params=_params(("parallel",)),
        name="final_norm",
    )(h, g.reshape(1, D))


def kernel(x, p, norm_mix_g, w_in, pool_w, pool_scale, w_pool_out, short_conv_w, short_conv_b, filt_w1, filt_b1, filt_w2, filt_b2, filt_w3, filt_freq, filt_bias, w_hyena_out, w_o, norm_ffn_g, peer_wq, peer_k1, peer_k2, peer_u, peer_v, norm_ple_g, ple_w_gate, ple_w_proj, final_norm_g):
    B, L, D = x.shape
    T = B * L
    depth = w_in.shape[0]
    pw = pool_scale.shape[-1]
    C = w_hyena_out.shape[1]
    hy_col, gate_col = pw, pw + 3 * C
    fwd, inv = _dft_tables(L, _tile(L, 512))
    h = x.reshape(T, D)
    for i in range(depth):
        proj = _inproj(h, norm_mix_g[i], w_in[i], 2 * D).reshape(B, L, -1)
        ga = _pool_branch(proj, pool_w[i], pool_scale[i], w_pool_out[i], gate_col)
        hs, hd, nq = _hyena_filters(L, filt_w1[i], filt_b1[i], filt_w2[i], filt_b2[i], filt_w3[i], filt_freq[i], C)
        gr, gi = _filter_spectrum(fwd, hs, hd)
        z2 = _hyena_operator(proj, short_conv_w[i], short_conv_b[i], fwd, inv, gr, gi, nq, filt_bias[i], hy_col, C)
        h1, hn, hnt = _merge(ga.reshape(T, D), z2.reshape(T, C), proj.reshape(T, -1), gate_col + D,
                             w_hyena_out[i], w_o[i], h, norm_ffn_g[i])
        s2, e2, d, e1 = _peer_route(hn, peer_wq[i], peer_k1[i], peer_k2[i])
        h2 = _peer_experts(hnt, peer_u[i], peer_v[i], s2, e2, d, e1, h1)
        h = _ple(h2, norm_ple_g[i], ple_w_gate[i], p[i].reshape(T, -1), ple_w_proj[i])
    return _final_norm(h, final_norm_g).reshape(B, L, D)
```

```python
import functools
import math

import jax
import jax.numpy as jnp
from jax import lax
from jax.experimental import pallas as pl
from jax.experimental.pallas import tpu as pltpu

F32 = jnp.float32
BF16 = jnp.bfloat16

NORM_EPS = 1e-6
POOL_WINDOWS = (2, 4, 8, 16)
SHORT_CONV = 3
FILTER_EMB = 33
DECAY_TARGET = 1e-2
DECAY_MIN = math.log(DECAY_TARGET) / 1.5
DECAY_MAX = math.log(DECAY_TARGET) / 0.3
PEER_TOPK = 16
LANES = 128
V7X_VMEM_BYTES = 64 * 1024 * 1024
VMEM_LIMIT = V7X_VMEM_BYTES - 8 * 1024 * 1024
HI = lax.Precision.HIGHEST


def _tile(n, pref):
    return pref if n % pref == 0 else n


def _params(sem, **flags):
    return pltpu.CompilerParams(dimension_semantics=sem, vmem_limit_bytes=VMEM_LIMIT, flags=flags or None)


def _bdot(a, b):
    return jnp.dot(a.astype(BF16), b.astype(BF16), preferred_element_type=F32)


def _rms(x, g):
    ms = jnp.mean(x * x, axis=-1, keepdims=True)
    return x * lax.rsqrt(ms + NORM_EPS) * g


def _inproj_kernel(x_ref, g_ref, w_ref, o_ref, xn_ref, *, n_plain):
    j = pl.program_id(1)

    @pl.when(j == 0)
    def _():
        xn_ref[...] = _rms(x_ref[...], g_ref[...]).astype(BF16)

    acc = jnp.dot(xn_ref[...], w_ref[0].astype(BF16), preferred_element_type=F32)

    @pl.when(j < n_plain)
    def _():
        o_ref[...] = acc.astype(o_ref.dtype)

    @pl.when(j >= n_plain)
    def _():
        o_ref[...] = jax.nn.sigmoid(acc).astype(o_ref.dtype)


def _inproj(h, g, w, layer, n_gate_cols):
    T, D = h.shape
    N = w.shape[2]
    tm, tn = _tile(T, 1024), _tile(N, 512)
    assert (N - n_gate_cols) % tn == 0
    return pl.pallas_call(
        functools.partial(_inproj_kernel, n_plain=(N - n_gate_cols) // tn),
        grid=(T // tm, N // tn),
        in_specs=[pl.BlockSpec((tm, D), lambda i, j: (i, 0)),
                  pl.BlockSpec((1, D), lambda i, j: (0, 0)),
                  pl.BlockSpec((1, D, tn), lambda i, j: (layer, 0, j))],
        out_specs=pl.BlockSpec((tm, tn), lambda i, j: (i, j)),
        out_shape=jax.ShapeDtypeStruct((T, N), BF16),
        scratch_shapes=[pltpu.VMEM((tm, D), BF16)],
        compiler_params=_params(("parallel", "arbitrary")),
        name="inproj",
    )(h, g.reshape(1, D), w)


def _pool_kernel(pin_ref, gate_ref, pw_ref, ps_ref, wo_ref, o_ref, *, tm, kw, seq):
    t0 = pl.program_id(1) * tm
    k0 = pl.multiple_of(jnp.clip(t0 - (kw - tm) // 2, 0, seq - kw), LANES)
    r = t0 + lax.broadcasted_iota(jnp.int32, (tm, 1), 0)
    s = k0 + lax.broadcasted_iota(jnp.int32, (1, kw), 1)
    group = pw_ref.shape[2]
    parts = []
    for gi, win in enumerate(POOL_WINDOWS):
        lo = jnp.clip(r - win // 2, 0, seq - 1)
        hi = jnp.clip(r + win - win // 2 - 1, 0, seq - 1)
        inv = 1.0 / (hi - lo + 1).astype(F32)
        band = jnp.where((s >= lo) & (s <= hi), inv, 0.0) - jnp.where(s == r, 1.0, 0.0)
        u = pin_ref[0, pl.ds(k0, kw), gi * group:(gi + 1) * group]
        centred = jnp.dot(band.astype(BF16), u, preferred_element_type=F32)
        parts.append(_bdot(centred, pw_ref[0, gi]))
    pooled = jnp.concatenate(parts, axis=-1) * ps_ref[...]
    a = _bdot(pooled, wo_ref[0])
    o_ref[0] = (a * gate_ref[0].astype(F32)).astype(o_ref.dtype)


def _layer_spec(a, layer):
    nd = a.ndim - 1
    return pl.BlockSpec((1,) + a.shape[1:], lambda *_: (layer,) + (0,) * nd, pipeline_mode=pl.Buffered(1))


def _pool_branch(proj, pool_w16, pool_scale, w_pool_out16, layer, gate_col):
    B, L, _ = proj.shape
    _, pw, D = w_pool_out16.shape
    tm = _tile(L, 256)
    kw = min(L, 2 * tm)
    assert max(POOL_WINDOWS) <= (kw - tm) // 2 or kw == L
    return pl.pallas_call(
        functools.partial(_pool_kernel, tm=tm, kw=kw, seq=L),
        grid=(B, L // tm),
        in_specs=[pl.BlockSpec((1, L, pw), lambda b, i: (b, 0, 0)),
                  pl.BlockSpec((1, tm, D), lambda b, i: (b, i, gate_col // D)),
                  _layer_spec(pool_w16, layer),
                  pl.BlockSpec((1, pw), lambda b, i: (0, 0)),
                  _layer_spec(w_pool_out16, layer)],
        out_specs=pl.BlockSpec((1, tm, D), lambda b, i: (b, i, 0)),
        out_shape=jax.ShapeDtypeStruct((B, L, D), BF16),
        compiler_params=_params(("parallel", "arbitrary")),
        name="pool_branch",
    )(proj, proj, pool_w16, pool_scale.reshape(1, pw), w_pool_out16)


def _filter_kernel(fb_ref, w1t_ref, w1c_ref, w1s_ref, b1_ref, w2_ref, b2_ref, fr_ref, dl_ref,
                   w3a_ref, w3b_ref, w3c_ref, w3d_ref, hs_ref, hd_ref, nq_ref, hdn_ref, *, seq):
    @pl.when(pl.program_id(0) == 0)
    def _():
        ti = lax.broadcasted_iota(jnp.int32, (seq, 1), 0).astype(F32)
        t = ti / (seq - 1.0)
        ang = (2.0 * math.pi / seq) * ti * fb_ref[...]
        z = (t * w1t_ref[...] + jnp.dot(jnp.cos(ang), w1c_ref[...], precision=HI, preferred_element_type=F32)
             - jnp.dot(jnp.sin(ang), w1s_ref[...], precision=HI, preferred_element_type=F32))
        hdn = jnp.sin(fr_ref[...] * (z + b1_ref[...]))
        hdn = jnp.sin(fr_ref[...] * (jnp.dot(hdn, w2_ref[...], precision=HI, preferred_element_type=F32)
                                    + b2_ref[...]))
        hdn_ref[...] = hdn

    ti = lax.broadcasted_iota(jnp.int32, (seq, 1), 0)
    t = ti.astype(F32) / (seq - 1.0)
    decay = jnp.exp(-t * dl_ref[...])
    sign = jnp.where(ti % 2 == 0, 1.0, -1.0)

    def filt(w3_ref):
        f = jnp.dot(hdn_ref[...], w3_ref[...], precision=HI, preferred_element_type=F32) * decay
        return f / (jnp.sum(jnp.abs(f), axis=0, keepdims=True) + 1e-6)

    for order, (wf_ref, wb_ref) in enumerate(((w3a_ref, w3c_ref), (w3b_ref, w3d_ref))):
        hf, hb = filt(wf_ref), filt(wb_ref)
        hsum = hf + hb
        hs_ref[order] = hsum.astype(hs_ref.dtype)
        hd_ref[order] = (hf - hb).astype(hd_ref.dtype)
        nq_ref[order] = jnp.sum(hsum * sign, axis=0, keepdims=True)


def _pad_to(a, shape):
    return jnp.pad(a, [(0, s - d) for d, s in zip(a.shape, shape)])


def _hyena_filters(seq, w1, b1, w2, b2, w3, freq, width):
    hid = w1.shape[1]
    bands = (FILTER_EMB - 1) // 2
    hp = max(LANES, hid)
    fb = _pad_to(jnp.linspace(1e-4, bands - 1, bands, dtype=F32)[None, :], (1, LANES))
    w1t = _pad_to(w1[0:1], (1, hp))
    w1c = _pad_to(w1[1:1 + bands], (LANES, hp))
    w1s = _pad_to(w1[1 + bands:], (LANES, hp))
    b1p, b2p, frp = (_pad_to(v.reshape(1, hid), (1, hp)) for v in (b1, b2, freq))
    w2p = _pad_to(w2, (hp, hp))
    w3p = _pad_to(w3, (hp, w3.shape[1]))
    deltas = jnp.abs(jnp.linspace(DECAY_MIN, DECAY_MAX, width, dtype=F32))[None, :]
    tc = _tile(width, 256)
    nc = width // tc
    small = lambda shape: pl.BlockSpec(shape, lambda c: (0,) * len(shape))
    w3spec = lambda g: pl.BlockSpec((hp, tc), lambda c, g=g: (0, g * nc + c))
    out3 = lambda rows: pl.BlockSpec((2, rows, tc), lambda c: (0, 0, c))
    return pl.pallas_call(
        functools.partial(_filter_kernel, seq=seq),
        grid=(nc,),
        in_specs=[small((1, LANES)), small((1, hp)), small((LANES, hp)), small((LANES, hp)), small((1, hp)),
                  small((hp, hp)), small((1, hp)), small((1, hp)), pl.BlockSpec((1, tc), lambda c: (0, c)),
                  w3spec(0), w3spec(1), w3spec(2), w3spec(3)],
        out_specs=[out3(seq), out3(seq), out3(1)],
        out_shape=[jax.ShapeDtypeStruct((2, seq, width), BF16), jax.ShapeDtypeStruct((2, seq, width), BF16),
                   jax.ShapeDtypeStruct((2, 1, width), F32)],
        scratch_shapes=[pltpu.VMEM((seq, hp), F32)],
        compiler_params=_params(("arbitrary",)),
        name="hyena_filters",
    )(fb, w1t, w1c, w1s, b1p, w2p, b2p, frp, deltas, w3p, w3p, w3p, w3p)


def _dft_tables(seq, fc):
    n = 2 * seq
    f = jnp.arange(seq, dtype=jnp.int32)[:, None]
    t = jnp.arange(seq, dtype=jnp.int32)[None, :]
    ang = ((f * t) & (n - 1)).astype(F32) * (2.0 * math.pi / n)
    cosm = jnp.cos(ang)
    sinm = jnp.where(f == 0, jnp.where(t % 2 == 0, 1.0, -1.0), -jnp.sin(ang))
    nk = seq // fc
    fwd = jnp.concatenate([cosm.reshape(nk, fc, seq), sinm.reshape(nk, fc, seq)], axis=1)
    inv = jnp.transpose(fwd, (0, 2, 1)) * (2.0 / n)
    return fwd.astype(BF16), inv.astype(BF16)


def _spectrum_kernel(w_ref, hs_ref, hd_ref, gr_ref, gi_ref, *, fc):
    gr_ref[0] = jnp.dot(w_ref[0, :fc, :], hs_ref[0], preferred_element_type=F32)
    gi_ref[0] = jnp.dot(w_ref[0, fc:, :], hd_ref[0], preferred_element_type=F32)


def _filter_spectrum(fwd, hs, hd):
    nk, fc2, seq = fwd.shape
    fc = fc2 // 2
    order, _, width = hs.shape
    tc = _tile(width, 512)
    hspec = pl.BlockSpec((1, seq, tc), lambda k, o, c: (o, 0, c))
    gspec = pl.BlockSpec((1, fc, tc), lambda k, o, c: (o, k, c))
    gshape = jax.ShapeDtypeStruct((order, seq, width), F32)
    return pl.pallas_call(
        functools.partial(_spectrum_kernel, fc=fc),
        grid=(nk, order, width // tc),
        in_specs=[pl.BlockSpec((1, fc2, seq), lambda k, o, c: (k, 0, 0)), hspec, hspec],
        out_specs=[gspec, gspec],
        out_shape=[gshape, gshape],
        compiler_params=_params(("arbitrary", "arbitrary", "arbitrary")),
        name="filter_spectrum",
    )(fwd, hs, hd)


def _hyena_kernel(v_ref, x1_ref, x2_ref, cwv_ref, cw1_ref, cw2_ref, cbv_ref, cb1_ref, cb2_ref,
                  fwd_ref, inv_ref, gr_ref, gi_ref, nq_ref, fbias_ref, o_ref,
                  u_ref, ub_ref, x1c_ref, x2c_ref, acc_ref, *, seq, fc):
    o, k = pl.program_id(2), pl.program_id(3)
    last = pl.num_programs(3) - 1
    row = lax.broadcasted_iota(jnp.int32, (seq, 1), 0)

    def short_conv(x_ref, w_ref, b_ref):
        x = x_ref[0].astype(F32)
        prev = jnp.where(row == 0, 0.0, pltpu.roll(x, 1, 0))
        nxt = jnp.where(row == seq - 1, 0.0, pltpu.roll(x, seq - 1, 0))
        return prev * w_ref[0:1, :] + x * w_ref[1:2, :] + nxt * w_ref[2:3, :] + b_ref[...]

    @pl.when((o == 0) & (k == 0))
    def _():
        v = short_conv(v_ref, cwv_ref, cbv_ref)
        u_ref[...] = v
        ub_ref[...] = v.astype(BF16)
        x1c_ref[...] = short_conv(x1_ref, cw1_ref, cb1_ref)
        x2c_ref[...] = short_conv(x2_ref, cw2_ref, cb2_ref)

    @pl.when(k == 0)
    def _():
        acc_ref[...] = jnp.zeros_like(acc_ref)

    spec = jnp.dot(fwd_ref[0], ub_ref[...], preferred_element_type=F32)
    vr, vi = spec[:fc], spec[fc:]
    gr, gi = gr_ref[0], gi_ref[0]
    special = (lax.broadcasted_iota(jnp.int32, (fc, 1), 0) == 0) & (k == 0)
    yr = jnp.where(special, 0.5 * vr * gr, vr * gr - vi * gi)
    yi = jnp.where(special, 0.5 * vi * nq_ref[0], vr * gi + vi * gr)
    y = jnp.concatenate([yr, yi], axis=0).astype(BF16)
    acc_ref[...] += jnp.dot(inv_ref[0], y, preferred_element_type=F32)

    @pl.when((k == last) & (o == 0))
    def _():
        z = (acc_ref[...] + u_ref[...] * fbias_ref[0]) * x1c_ref[...]
        u_ref[...] = z
        ub_ref[...] = z.astype(BF16)

    @pl.when((k == last) & (o == 1))
    def _():
        o_ref[0] = ((acc_ref[...] + u_ref[...] * fbias_ref[0]) * x2c_ref[...]).astype(o_ref.dtype)


def _hyena_operator(proj, conv_w, conv_b, fwd, inv, gr, gi, nq, filt_bias, col0, width):
    B, L, _ = proj.shape
    nk, fc2, _ = fwd.shape
    tc = _tile(width, 256)
    nc = width // tc
    assert col0 % tc == 0
    xspec = lambda part: pl.BlockSpec((1, L, tc), lambda b, c, o, k, part=part: (b, 0, col0 // tc + part * nc + c))
    wspec = lambda rows, part: pl.BlockSpec((rows, tc), lambda b, c, o, k, part=part: (0, part * nc + c))
    ospec = lambda rows: pl.BlockSpec((1, rows, tc), lambda b, c, o, k: (o, 0, c))
    cb = conv_b.reshape(1, -1)
    return pl.pallas_call(
        functools.partial(_hyena_kernel, seq=L, fc=fc2 // 2),
        grid=(B, nc, 2, nk),
        in_specs=[xspec(0), xspec(1), xspec(2), wspec(SHORT_CONV, 0), wspec(SHORT_CONV, 1), wspec(SHORT_CONV, 2),
                  wspec(1, 0), wspec(1, 1), wspec(1, 2),
                  pl.BlockSpec((1, fc2, L), lambda b, c, o, k: (k, 0, 0)),
                  pl.BlockSpec((1, L, fc2), lambda b, c, o, k: (k, 0, 0)),
                  pl.BlockSpec((1, fc2 // 2, tc), lambda b, c, o, k: (o, k, c)),
                  pl.BlockSpec((1, fc2 // 2, tc), lambda b, c, o, k: (o, k, c)),
                  ospec(1), ospec(1)],
        out_specs=pl.BlockSpec((1, L, tc), lambda b, c, o, k: (b, 0, c)),
        out_shape=jax.ShapeDtypeStruct((B, L, width), BF16),
        scratch_shapes=[pltpu.VMEM((L, tc), F32), pltpu.VMEM((L, tc), BF16), pltpu.VMEM((L, tc), F32),
                        pltpu.VMEM((L, tc), F32), pltpu.VMEM((L, tc), F32)],
        compiler_params=_params(("arbitrary",) * 4),
        name="hyena_operator",
    )(proj, proj, proj, conv_w, conv_w, conv_w, cb, cb, cb, fwd, inv, gr, gi, nq,
      filt_bias.reshape(2, 1, width))


def _merge_kernel(ga_ref, z_ref, gate_ref, wh_ref, wo_ref, h_ref, g_ref, h1_ref, hn_ref, hnt_ref):
    hy = jnp.dot(z_ref[...], wh_ref[0], preferred_element_type=F32)
    mix = ga_ref[...].astype(F32) + gate_ref[...].astype(F32) * hy
    h1 = h_ref[...] + _bdot(mix, wo_ref[0])
    h1_ref[...] = h1
    hn = _rms(h1, g_ref[...])
    hn_ref[...] = hn.astype(BF16)
    hnt_ref[...] = hn.T.astype(BF16)


def _merge(ga, z2, proj, gate_col, whb, wob, layer, h, g):
    T, D = h.shape
    C = z2.shape[1]
    tm = _tile(T, 256)
    row = lambda w: pl.BlockSpec((tm, w), lambda i: (i, 0))
    full = lambda a: pl.BlockSpec(a.shape, lambda i: (0,) * a.ndim)
    g2 = g.reshape(1, D)
    return pl.pallas_call(
        _merge_kernel,
        grid=(T // tm,),
        in_specs=[row(D), row(C), pl.BlockSpec((tm, D), lambda i: (i, gate_col // D)),
                  _layer_spec(whb, layer), _layer_spec(wob, layer), row(D), full(g2)],
        out_specs=[row(D), row(D), pl.BlockSpec((D, tm), lambda i: (0, i))],
        out_shape=[jax.ShapeDtypeStruct((T, D), F32), jax.ShapeDtypeStruct((T, D), BF16),
                   jax.ShapeDtypeStruct((D, T), BF16)],
        compiler_params=_params(("parallel",)),
        name="mixer_merge",
    )(ga, z2, proj, whb, wob, h, g2)


UNRANKED = 127.0


def _extract_top(x, count, with_rank=False):
    out = []
    rank = jnp.full(x.shape, UNRANKED, F32) if with_rank else None
    for k in range(count):
        m = jnp.max(x, axis=0, keepdims=True)
        out.append(m)
        hit = x == m
        if with_rank:
            rank = jnp.where(hit, float(k), rank)
        x = jnp.where(hit, -jnp.inf, x)
    return (out, rank) if with_rank else out


def _route_kernel(hn_ref, wq_ref, k1_ref, k2_ref, r2_ref, e2_ref, cnt_ref, e1_ref, q_ref, *, heads, half):
    q_ref[...] = jnp.dot(hn_ref[...], wq_ref[0], preferred_element_type=F32)
    k1, k2 = k1_ref[0].astype(BF16), k2_ref[0].astype(BF16)
    nt = (((1,), (1,)), ((), ()))
    nbest = PEER_TOPK + 1
    span = [-(-(nbest // (a + 1)) // 8) * 8 for a in range(nbest)]

    def body(h, carry):
        base = pl.multiple_of(h * 2 * half, 2 * half)
        q1 = q_ref[:, pl.ds(base, half)].astype(BF16)
        q2 = q_ref[:, pl.ds(base + half, half)].astype(BF16)
        s1 = lax.dot_general(k1, q1, nt, preferred_element_type=F32)
        s2 = lax.dot_general(k2, q2, nt, preferred_element_type=F32)
        v1 = _extract_top(s1, nbest)
        v2, rank2 = _extract_top(s2, nbest, with_rank=True)
        v2p = jnp.concatenate(v2 + [jnp.full_like(v2[0], -jnp.inf)] * (span[0] - nbest), axis=0)
        cand = jnp.concatenate([v1[a] + v2p[:span[a]] for a in range(nbest)], axis=0)
        best = _extract_top(cand, nbest)
        top = best[0]
        z = sum(jnp.exp(b - top) for b in best[1:PEER_TOPK]) + 1.0
        thr = 0.5 * (best[PEER_TOPK - 1] + best[PEER_TOPK])
        need = thr - s1
        r2_ref[h] = rank2.astype(r2_ref.dtype)
        e2_ref[h] = (jnp.exp(s2 - v2[0]) / z).astype(e2_ref.dtype)
        cnt_ref[h] = sum(jnp.where(v >= need, 1.0, 0.0) for v in v2)
        e1_ref[h] = jnp.exp(s1 - v1[0])
        return carry

    lax.fori_loop(0, heads, body, 0)


def _peer_route(hn, wqb, k1, k2, layer):
    T, D = hn.shape
    _, nk, half = k1.shape
    heads = wqb.shape[2] // (2 * half)
    tm = _tile(T, 512)
    ospec = pl.BlockSpec((heads, nk, tm), lambda i: (0, 0, i))
    wspec = lambda a: pl.BlockSpec((1,) + a.shape[1:], lambda i: (layer, 0, 0), pipeline_mode=pl.Buffered(1))
    return pl.pallas_call(
        functools.partial(_route_kernel, heads=heads, half=half),
        grid=(T // tm,),
        in_specs=[pl.BlockSpec((tm, D), lambda i: (i, 0)), wspec(wqb), wspec(k1), wspec(k2)],
        out_specs=[ospec] * 4,
        out_shape=[jax.ShapeDtypeStruct((heads, nk, T), F32)] * 4,
        scratch_shapes=[pltpu.VMEM((tm, wqb.shape[2]), F32)],
        compiler_params=_params(("parallel",)),
        name="peer_route",
    )(hn, wqb, k1, k2)


def _experts_kernel(hnt_ref, u_ref, v_ref, r2_ref, e2_ref, cnt_ref, e1_ref, h1_ref, o_ref, w_ref,
                    *, heads, nk, tm, te):
    j = pl.program_id(1)

    @pl.when(j == 0)
    def _():
        o_ref[...] = h1_ref[...]

    groups = te // nk
    g0 = pl.multiple_of(j * groups, groups)
    na, rc = 4, 64

    for tb in range(tm // LANES):
        lanes = slice(tb * LANES, (tb + 1) * LANES)
        for c in range(nk // rc):
            for ag in range(groups // na):
                w = [jnp.zeros((rc, LANES), F32) for _ in range(na)]
                for h in range(heads):
                    r2 = r2_ref[h, c * rc:(c + 1) * rc, lanes]
                    e2 = e2_ref[h, c * rc:(c + 1) * rc, lanes]
                    cg = cnt_ref[h, pl.ds(g0, groups), lanes]
                    e1g = e1_ref[h, pl.ds(g0, groups), lanes]
                    for a in range(na):
                        i = ag * na + a
                        w[a] = w[a] + jnp.where(r2 < cg[i:i + 1], e2, 0.0) * e1g[i:i + 1]
                for a in range(na):
                    r0 = (ag * na + a) * nk + c * rc
                    w_ref[r0:r0 + rc, lanes] = w[a]

    hid = jnp.dot(u_ref[0], hnt_ref[...], preferred_element_type=F32)
    at = 0.5 * hid * (1.0 + lax.erf(hid * (1.0 / math.sqrt(2.0)))) * w_ref[...]
    o_ref[...] += jnp.dot(at.T.astype(BF16), v_ref[0], preferred_element_type=F32)


def _peer_experts(hnt, u16, v16, r2, e2, cnt, e1, h1, layer):
    D, T = hnt.shape
    E = u16.shape[1]
    heads, nk, _ = r2.shape
    tm, te = _tile(T, 512), _tile(E, 8 * nk)
    assert te == 8 * nk
    once = dict(pipeline_mode=pl.Buffered(1))
    rspec = pl.BlockSpec((heads, nk, tm), lambda i, j: (0, 0, i), **once)
    tspec = pl.BlockSpec((1, te, D), lambda i, j: (layer, j, 0))
    return pl.pallas_call(
        functools.partial(_experts_kernel, heads=heads, nk=nk, tm=tm, te=te),
        grid=(T // tm, E // te),
        in_specs=[pl.BlockSpec((D, tm), lambda i, j: (0, i), **once), tspec, tspec,
                  rspec, rspec, rspec, rspec, pl.BlockSpec((tm, D), lambda i, j: (i, 0), **once)],
        out_specs=pl.BlockSpec((tm, D), lambda i, j: (i, 0)),
        out_shape=jax.ShapeDtypeStruct((T, D), F32),
        scratch_shapes=[pltpu.VMEM((te, tm), F32)],
        compiler_params=_params(("parallel", "arbitrary")),
        name="peer_experts",
    )(hnt, u16, v16, r2, e2, cnt, e1, h1)


def _ple_kernel(h_ref, g_ref, wg_ref, p_ref, wp_ref, gf_ref, o_ref, *, final):
    h = h_ref[...]
    xn = _rms(h, g_ref[...]).astype(BF16)
    gate = jax.nn.sigmoid(jnp.dot(xn, wg_ref[0], preferred_element_type=F32))
    out = h + gate * _bdot(p_ref[0], wp_ref[0])
    o_ref[...] = _rms(out, gf_ref[...]) if final else out


def _ple(h, g, w_gate16, p, w_proj16, layer, final_g=None):
    T, D = h.shape
    pd = p.shape[2]
    tm = _tile(T, 512)
    vec = pl.BlockSpec((1, D), lambda i: (0, 0))
    gf = (g if final_g is None else final_g).reshape(1, D)
    return pl.pallas_call(
        functools.partial(_ple_kernel, final=final_g is not None),
        grid=(T // tm,),
        in_specs=[pl.BlockSpec((tm, D), lambda i: (i, 0)), vec, _layer_spec(w_gate16, layer),
                  pl.BlockSpec((1, tm, pd), lambda i: (layer, i, 0)), _layer_spec(w_proj16, layer), vec],
        out_specs=pl.BlockSpec((tm, D), lambda i: (i, 0)),
        out_shape=jax.ShapeDtypeStruct((T, D), F32),
        compiler_params=_params(("parallel",)),
        name="ple",
    )(h, g.reshape(1, D), w_gate16, p, w_proj16, gf)


def kernel(x, p, norm_mix_g, w_in, pool_w, pool_scale, w_pool_out, short_conv_w, short_conv_b, filt_w1, filt_b1, filt_w2, filt_b2, filt_w3, filt_freq, filt_bias, w_hyena_out, w_o, norm_ffn_g, peer_wq, peer_k1, peer_k2, peer_u, peer_v, norm_ple_g, ple_w_gate, ple_w_proj, final_norm_g):
    B, L, D = x.shape
    T = B * L
    depth = w_in.shape[0]
    pw = pool_scale.shape[-1]
    C = w_hyena_out.shape[1]
    hy_col, gate_col = pw, pw + 3 * C
    fwd, inv = _dft_tables(L, _tile(L, 512))
    pool_w16, wpo16, why16, wo16, wq16, u16, v16, wg16, wp16 = (
        w.astype(BF16) for w in (pool_w, w_pool_out, w_hyena_out, w_o, peer_wq, peer_u, peer_v,
                                 ple_w_gate, ple_w_proj))
    p3 = p.reshape(depth, T, -1)
    h = x.reshape(T, D)
    for i in range(depth):
        proj = _inproj(h, norm_mix_g[i], w_in, i, 2 * D).reshape(B, L, -1)
        ga = _pool_branch(proj, pool_w16, pool_scale[i], wpo16, i, gate_col)
        hs, hd, nq = _hyena_filters(L, filt_w1[i], filt_b1[i], filt_w2[i], filt_b2[i], filt_w3[i], filt_freq[i], C)
        gr, gi = _filter_spectrum(fwd, hs, hd)
        z2 = _hyena_operator(proj, short_conv_w[i], short_conv_b[i], fwd, inv, gr, gi, nq, filt_bias[i], hy_col, C)
        h1, hn, hnt = _merge(ga.reshape(T, D), z2.reshape(T, C), proj.reshape(T, -1), gate_col + D,
                             why16, wo16, i, h, norm_ffn_g[i])
        r2, e2, cnt, e1 = _peer_route(hn, wq16, peer_k1, peer_k2, i)
        h2 = _peer_experts(hnt, u16, v16, r2, e2, cnt, e1, h1, i)
        h = _ple(h2, norm_ple_g[i], wg16, p3, wp16, i, final_norm_g if i == depth - 1 else None)
    return h.reshape(B, L, D)
```

```python
import functools
import math

import jax
import jax.numpy as jnp
from jax import lax
from jax.experimental import pallas as pl
from jax.experimental.pallas import tpu as pltpu

F32 = jnp.float32
BF16 = jnp.bfloat16

NORM_EPS = 1e-6
POOL_WINDOWS = (2, 4, 8, 16)
SHORT_CONV = 3
FILTER_EMB = 33
DECAY_TARGET = 1e-2
DECAY_MIN = math.log(DECAY_TARGET) / 1.5
DECAY_MAX = math.log(DECAY_TARGET) / 0.3
PEER_TOPK = 16
LANES = 128
V7X_VMEM_BYTES = 64 * 1024 * 1024
VMEM_LIMIT = V7X_VMEM_BYTES - 8 * 1024 * 1024
HI = lax.Precision.HIGHEST


def _tile(n, pref):
    return pref if n % pref == 0 else n


def _params(sem, **flags):
    return pltpu.CompilerParams(dimension_semantics=sem, vmem_limit_bytes=VMEM_LIMIT, flags=flags or None)


def _bdot(a, b):
    return jnp.dot(a.astype(BF16), b.astype(BF16), preferred_element_type=F32)


def _rms(x, g):
    ms = jnp.mean(x * x, axis=-1, keepdims=True)
    return x * lax.rsqrt(ms + NORM_EPS) * g


def _inproj_kernel(x_ref, g_ref, w_ref, o_ref, xn_ref, *, n_plain):
    j = pl.program_id(1)

    @pl.when(j == 0)
    def _():
        xn_ref[...] = _rms(x_ref[...], g_ref[...]).astype(BF16)

    acc = jnp.dot(xn_ref[...], w_ref[0].astype(BF16), preferred_element_type=F32)
    o_ref[...] = jnp.where(j >= n_plain, jax.nn.sigmoid(acc), acc).astype(o_ref.dtype)


def _inproj(h, g, w, layer, n_gate_cols):
    T, D = h.shape
    N = w.shape[2]
    tm, tn = _tile(T, 1024), _tile(N, 512)
    assert (N - n_gate_cols) % tn == 0
    return pl.pallas_call(
        functools.partial(_inproj_kernel, n_plain=(N - n_gate_cols) // tn),
        grid=(T // tm, N // tn),
        in_specs=[pl.BlockSpec((tm, D), lambda i, j: (i, 0)),
                  pl.BlockSpec((1, D), lambda i, j: (0, 0)),
                  pl.BlockSpec((1, D, tn), lambda i, j: (layer, 0, j))],
        out_specs=pl.BlockSpec((tm, tn), lambda i, j: (i, j)),
        out_shape=jax.ShapeDtypeStruct((T, N), BF16),
        scratch_shapes=[pltpu.VMEM((tm, D), BF16)],
        compiler_params=_params(("parallel", "arbitrary")),
        name="inproj",
    )(h, g.reshape(1, D), w)


def _pool_kernel(pin_ref, gate_ref, pw_ref, ps_ref, wo_ref, o_ref, *, tm, kw, seq):
    t0 = pl.program_id(1) * tm
    k0 = pl.multiple_of(jnp.clip(t0 - (kw - tm) // 2, 0, seq - kw), LANES)
    r = t0 + lax.broadcasted_iota(jnp.int32, (tm, 1), 0)
    s = k0 + lax.broadcasted_iota(jnp.int32, (1, kw), 1)
    group = pw_ref.shape[2]
    parts = []
    for gi, win in enumerate(POOL_WINDOWS):
        lo = jnp.clip(r - win // 2, 0, seq - 1)
        hi = jnp.clip(r + win - win // 2 - 1, 0, seq - 1)
        inv = 1.0 / (hi - lo + 1).astype(F32)
        band = jnp.where((s >= lo) & (s <= hi), inv, 0.0) - jnp.where(s == r, 1.0, 0.0)
        u = pin_ref[0, pl.ds(k0, kw), gi * group:(gi + 1) * group]
        centred = jnp.dot(band.astype(BF16), u, preferred_element_type=F32)
        parts.append(_bdot(centred, pw_ref[0, gi]))
    pooled = jnp.concatenate(parts, axis=-1) * ps_ref[...]
    a = _bdot(pooled, wo_ref[0])
    o_ref[0] = (a * gate_ref[0].astype(F32)).astype(o_ref.dtype)


def _layer_spec(a, layer):
    nd = a.ndim - 1
    return pl.BlockSpec((1,) + a.shape[1:], lambda *_: (layer,) + (0,) * nd, pipeline_mode=pl.Buffered(1))


def _pool_branch(proj, pool_w16, pool_scale, w_pool_out16, layer, gate_col):
    B, L, _ = proj.shape
    _, pw, D = w_pool_out16.shape
    tm = _tile(L, 256)
    kw = min(L, 2 * tm)
    assert max(POOL_WINDOWS) <= (kw - tm) // 2 or kw == L
    return pl.pallas_call(
        functools.partial(_pool_kernel, tm=tm, kw=kw, seq=L),
        grid=(B, L // tm),
        in_specs=[pl.BlockSpec((1, L, pw), lambda b, i: (b, 0, 0)),
                  pl.BlockSpec((1, tm, D), lambda b, i: (b, i, gate_col // D)),
                  _layer_spec(pool_w16, layer),
                  pl.BlockSpec((1, pw), lambda b, i: (0, 0)),
                  _layer_spec(w_pool_out16, layer)],
        out_specs=pl.BlockSpec((1, tm, D), lambda b, i: (b, i, 0)),
        out_shape=jax.ShapeDtypeStruct((B, L, D), BF16),
        compiler_params=_params(("parallel", "arbitrary")),
        name="pool_branch",
    )(proj, proj, pool_w16, pool_scale.reshape(1, pw), w_pool_out16)


def _filter_kernel(fb_ref, w1t_ref, w1c_ref, w1s_ref, b1_ref, w2_ref, b2_ref, fr_ref, dl_ref,
                   w3a_ref, w3b_ref, w3c_ref, w3d_ref, hs_ref, hd_ref, nq_ref, hdn_ref, *, seq):
    @pl.when(pl.program_id(0) == 0)
    def _():
        ti = lax.broadcasted_iota(jnp.int32, (seq, 1), 0).astype(F32)
        t = ti / (seq - 1.0)
        ang = (2.0 * math.pi / seq) * ti * fb_ref[...]
        z = (t * w1t_ref[...] + jnp.dot(jnp.cos(ang), w1c_ref[...], precision=HI, preferred_element_type=F32)
             - jnp.dot(jnp.sin(ang), w1s_ref[...], precision=HI, preferred_element_type=F32))
        hdn = jnp.sin(fr_ref[...] * (z + b1_ref[...]))
        hdn = jnp.sin(fr_ref[...] * (jnp.dot(hdn, w2_ref[...], precision=HI, preferred_element_type=F32)
                                    + b2_ref[...]))
        hdn_ref[...] = hdn

    ti = lax.broadcasted_iota(jnp.int32, (seq, 1), 0)
    t = ti.astype(F32) / (seq - 1.0)
    decay = jnp.exp(-t * dl_ref[...])
    sign = jnp.where(ti % 2 == 0, 1.0, -1.0)

    def filt(w3_ref):
        f = jnp.dot(hdn_ref[...], w3_ref[...], precision=HI, preferred_element_type=F32) * decay
        return f / (jnp.sum(jnp.abs(f), axis=0, keepdims=True) + 1e-6)

    for order, (wf_ref, wb_ref) in enumerate(((w3a_ref, w3c_ref), (w3b_ref, w3d_ref))):
        hf, hb = filt(wf_ref), filt(wb_ref)
        hsum = hf + hb
        hs_ref[order] = hsum.astype(hs_ref.dtype)
        hd_ref[order] = (hf - hb).astype(hd_ref.dtype)
        nq_ref[order] = jnp.sum(hsum * sign, axis=0, keepdims=True)


def _pad_to(a, shape):
    return jnp.pad(a, [(0, s - d) for d, s in zip(a.shape, shape)])


def _hyena_filters(seq, w1, b1, w2, b2, w3, freq, width):
    hid = w1.shape[1]
    bands = (FILTER_EMB - 1) // 2
    hp = max(LANES, hid)
    fb = _pad_to(jnp.linspace(1e-4, bands - 1, bands, dtype=F32)[None, :], (1, LANES))
    w1t = _pad_to(w1[0:1], (1, hp))
    w1c = _pad_to(w1[1:1 + bands], (LANES, hp))
    w1s = _pad_to(w1[1 + bands:], (LANES, hp))
    b1p, b2p, frp = (_pad_to(v.reshape(1, hid), (1, hp)) for v in (b1, b2, freq))
    w2p = _pad_to(w2, (hp, hp))
    w3p = _pad_to(w3, (hp, w3.shape[1]))
    deltas = jnp.abs(jnp.linspace(DECAY_MIN, DECAY_MAX, width, dtype=F32))[None, :]
    tc = _tile(width, 256)
    nc = width // tc
    small = lambda shape: pl.BlockSpec(shape, lambda c: (0,) * len(shape))
    w3spec = lambda g: pl.BlockSpec((hp, tc), lambda c, g=g: (0, g * nc + c))
    out3 = lambda rows: pl.BlockSpec((2, rows, tc), lambda c: (0, 0, c))
    return pl.pallas_call(
        functools.partial(_filter_kernel, seq=seq),
        grid=(nc,),
        in_specs=[small((1, LANES)), small((1, hp)), small((LANES, hp)), small((LANES, hp)), small((1, hp)),
                  small((hp, hp)), small((1, hp)), small((1, hp)), pl.BlockSpec((1, tc), lambda c: (0, c)),
                  w3spec(0), w3spec(1), w3spec(2), w3spec(3)],
        out_specs=[out3(seq), out3(seq), out3(1)],
        out_shape=[jax.ShapeDtypeStruct((2, seq, width), BF16), jax.ShapeDtypeStruct((2, seq, width), BF16),
                   jax.ShapeDtypeStruct((2, 1, width), F32)],
        scratch_shapes=[pltpu.VMEM((seq, hp), F32)],
        compiler_params=_params(("arbitrary",)),
        name="hyena_filters",
    )(fb, w1t, w1c, w1s, b1p, w2p, b2p, frp, deltas, w3p, w3p, w3p, w3p)


def _dft_tables(seq, fc):
    n = 2 * seq
    f = jnp.arange(seq, dtype=jnp.int32)[:, None]
    t = jnp.arange(seq, dtype=jnp.int32)[None, :]
    ang = ((f * t) & (n - 1)).astype(F32) * (2.0 * math.pi / n)
    cosm = jnp.cos(ang)
    sinm = jnp.where(f == 0, jnp.where(t % 2 == 0, 1.0, -1.0), -jnp.sin(ang))
    nk = seq // fc
    fwd = jnp.concatenate([cosm.reshape(nk, fc, seq), sinm.reshape(nk, fc, seq)], axis=1)
    inv = jnp.transpose(fwd, (0, 2, 1)) * (2.0 / n)
    return fwd.astype(BF16), inv.astype(BF16)


def _spectrum_kernel(w_ref, hs_ref, hd_ref, gr_ref, gi_ref, *, fc):
    gr_ref[0] = jnp.dot(w_ref[0, :fc, :], hs_ref[0], preferred_element_type=F32)
    gi_ref[0] = jnp.dot(w_ref[0, fc:, :], hd_ref[0], preferred_element_type=F32)


def _filter_spectrum(fwd, hs, hd):
    nk, fc2, seq = fwd.shape
    fc = fc2 // 2
    order, _, width = hs.shape
    tc = _tile(width, 512)
    hspec = pl.BlockSpec((1, seq, tc), lambda k, o, c: (o, 0, c))
    gspec = pl.BlockSpec((1, fc, tc), lambda k, o, c: (o, k, c))
    gshape = jax.ShapeDtypeStruct((order, seq, width), F32)
    return pl.pallas_call(
        functools.partial(_spectrum_kernel, fc=fc),
        grid=(nk, order, width // tc),
        in_specs=[pl.BlockSpec((1, fc2, seq), lambda k, o, c: (k, 0, 0)), hspec, hspec],
        out_specs=[gspec, gspec],
        out_shape=[gshape, gshape],
        compiler_params=_params(("arbitrary", "arbitrary", "arbitrary")),
        name="filter_spectrum",
    )(fwd, hs, hd)


def _hyena_kernel(v_ref, x1_ref, x2_ref, cwv_ref, cw1_ref, cw2_ref, cbv_ref, cb1_ref, cb2_ref,
                  fwd_ref, inv_ref, gr_ref, gi_ref, nq_ref, fbias_ref, o_ref,
                  u_ref, ub_ref, acc_ref, *, seq, fc):
    o, k = pl.program_id(2), pl.program_id(3)
    last = pl.num_programs(3) - 1
    row = lax.broadcasted_iota(jnp.int32, (seq, 1), 0)

    def short_conv(x_ref, w_ref, b_ref):
        x = x_ref[0].astype(F32)
        prev = jnp.where(row == 0, 0.0, pltpu.roll(x, 1, 0))
        nxt = jnp.where(row == seq - 1, 0.0, pltpu.roll(x, seq - 1, 0))
        return prev * w_ref[0:1, :] + x * w_ref[1:2, :] + nxt * w_ref[2:3, :] + b_ref[...]

    @pl.when((o == 0) & (k == 0))
    def _():
        v = short_conv(v_ref, cwv_ref, cbv_ref)
        u_ref[...] = v
        ub_ref[...] = v.astype(BF16)

    @pl.when(k == 0)
    def _():
        acc_ref[...] = jnp.zeros_like(acc_ref)

    spec = jnp.dot(fwd_ref[0], ub_ref[...], preferred_element_type=F32)
    vr, vi = spec[:fc], spec[fc:]
    gr, gi = gr_ref[0], gi_ref[0]
    special = (lax.broadcasted_iota(jnp.int32, (fc, 1), 0) == 0) & (k == 0)
    yr = jnp.where(special, 0.5 * vr * gr, vr * gr - vi * gi)
    yi = jnp.where(special, 0.5 * vi * nq_ref[0], vr * gi + vi * gr)
    y = jnp.concatenate([yr, yi], axis=0).astype(BF16)
    acc_ref[...] += jnp.dot(inv_ref[0], y, preferred_element_type=F32)

    @pl.when((k == last) & (o == 0))
    def _():
        z = (acc_ref[...] + u_ref[...] * fbias_ref[0]) * short_conv(x1_ref, cw1_ref, cb1_ref)
        u_ref[...] = z
        ub_ref[...] = z.astype(BF16)

    @pl.when((k == last) & (o == 1))
    def _():
        gate = short_conv(x2_ref, cw2_ref, cb2_ref)
        o_ref[0] = ((acc_ref[...] + u_ref[...] * fbias_ref[0]) * gate).astype(o_ref.dtype)


def _hyena_operator(proj, conv_w, conv_b, fwd, inv, gr, gi, nq, filt_bias, col0, width):
    B, L, _ = proj.shape
    nk, fc2, _ = fwd.shape
    tc = _tile(width, 512)
    nc = width // tc
    assert col0 % tc == 0
    xspec = lambda part: pl.BlockSpec((1, L, tc), lambda b, c, o, k, part=part: (b, 0, col0 // tc + part * nc + c),
                                      pipeline_mode=pl.Buffered(1))
    wspec = lambda rows, part: pl.BlockSpec((rows, tc), lambda b, c, o, k, part=part: (0, part * nc + c))
    ospec = lambda rows: pl.BlockSpec((1, rows, tc), lambda b, c, o, k: (o, 0, c))
    cb = conv_b.reshape(1, -1)
    return pl.pallas_call(
        functools.partial(_hyena_kernel, seq=L, fc=fc2 // 2),
        grid=(B, nc, 2, nk),
        in_specs=[xspec(0), xspec(1), xspec(2), wspec(SHORT_CONV, 0), wspec(SHORT_CONV, 1), wspec(SHORT_CONV, 2),
                  wspec(1, 0), wspec(1, 1), wspec(1, 2),
                  pl.BlockSpec((1, fc2, L), lambda b, c, o, k: (k, 0, 0)),
                  pl.BlockSpec((1, L, fc2), lambda b, c, o, k: (k, 0, 0)),
                  pl.BlockSpec((1, fc2 // 2, tc), lambda b, c, o, k: (o, k, c)),
                  pl.BlockSpec((1, fc2 // 2, tc), lambda b, c, o, k: (o, k, c)),
                  ospec(1), ospec(1)],
        out_specs=pl.BlockSpec((1, L, tc), lambda b, c, o, k: (b, 0, c)),
        out_shape=jax.ShapeDtypeStruct((B, L, width), BF16),
        scratch_shapes=[pltpu.VMEM((L, tc), F32), pltpu.VMEM((L, tc), BF16), pltpu.VMEM((L, tc), F32)],
        compiler_params=_params(("arbitrary",) * 4),
        name="hyena_operator",
    )(proj, proj, proj, conv_w, conv_w, conv_w, cb, cb, cb, fwd, inv, gr, gi, nq,
      filt_bias.reshape(2, 1, width))


def _merge_kernel(ga_ref, z_ref, gate_ref, wh_ref, wo_ref, h_ref, g_ref, h1_ref, hn_ref, hnt_ref):
    hy = jnp.dot(z_ref[...], wh_ref[0], preferred_element_type=F32)
    mix = ga_ref[...].astype(F32) + gate_ref[...].astype(F32) * hy
    h1 = h_ref[...] + _bdot(mix, wo_ref[0])
    h1_ref[...] = h1
    hn = _rms(h1, g_ref[...])
    hn_ref[...] = hn.astype(BF16)
    hnt_ref[...] = hn.T.astype(BF16)


def _merge(ga, z2, proj, gate_col, whb, wob, layer, h, g):
    T, D = h.shape
    C = z2.shape[1]
    tm = _tile(T, 256)
    row = lambda w: pl.BlockSpec((tm, w), lambda i: (i, 0))
    full = lambda a: pl.BlockSpec(a.shape, lambda i: (0,) * a.ndim)
    g2 = g.reshape(1, D)
    return pl.pallas_call(
        _merge_kernel,
        grid=(T // tm,),
        in_specs=[row(D), row(C), pl.BlockSpec((tm, D), lambda i: (i, gate_col // D)),
                  _layer_spec(whb, layer), _layer_spec(wob, layer), row(D), full(g2)],
        out_specs=[row(D), row(D), pl.BlockSpec((D, tm), lambda i: (0, i))],
        out_shape=[jax.ShapeDtypeStruct((T, D), F32), jax.ShapeDtypeStruct((T, D), BF16),
                   jax.ShapeDtypeStruct((D, T), BF16)],
        compiler_params=_params(("parallel",)),
        name="mixer_merge",
    )(ga, z2, proj, whb, wob, h, g2)


UNRANKED = 127.0


def _extract_top(x, count, with_rank=False):
    out = []
    rank = jnp.full(x.shape, UNRANKED, F32) if with_rank else None
    for k in range(count):
        m = jnp.max(x, axis=0, keepdims=True)
        out.append(m)
        hit = x == m
        if with_rank:
            rank = jnp.where(hit, float(k), rank)
        x = jnp.where(hit, -jnp.inf, x)
    return (out, rank) if with_rank else out


def _route_kernel(hn_ref, wq_ref, k1_ref, k2_ref, r2_ref, e2_ref, cnt_ref, e1_ref, q_ref, *, heads, half):
    q_ref[...] = jnp.dot(hn_ref[...], wq_ref[0], preferred_element_type=F32)
    k1, k2 = k1_ref[0].astype(BF16), k2_ref[0].astype(BF16)
    nt = (((1,), (1,)), ((), ()))
    nbest = PEER_TOPK + 1
    span = [-(-(nbest // (a + 1)) // 8) * 8 for a in range(nbest)]

    def route_head(h):
        base = pl.multiple_of(h * 2 * half, 2 * half)
        q1 = q_ref[:, pl.ds(base, half)].astype(BF16)
        q2 = q_ref[:, pl.ds(base + half, half)].astype(BF16)
        s1 = lax.dot_general(k1, q1, nt, preferred_element_type=F32)
        s2 = lax.dot_general(k2, q2, nt, preferred_element_type=F32)
        v1 = _extract_top(s1, nbest)
        v2, rank2 = _extract_top(s2, nbest, with_rank=True)
        v2p = jnp.concatenate(v2 + [jnp.full_like(v2[0], -jnp.inf)] * (span[0] - nbest), axis=0)
        cand = jnp.concatenate([v1[a] + v2p[:span[a]] for a in range(nbest)], axis=0)
        best = _extract_top(cand, nbest)
        top = best[0]
        z = sum(jnp.exp(b - top) for b in best[1:PEER_TOPK]) + 1.0
        thr = 0.5 * (best[PEER_TOPK - 1] + best[PEER_TOPK])
        need = thr - s1
        r2_ref[h] = rank2.astype(r2_ref.dtype)
        e2_ref[h] = (jnp.exp(s2 - v2[0]) / z).astype(e2_ref.dtype)
        cnt_ref[h] = sum(jnp.where(v >= need, 1.0, 0.0) for v in v2)
        e1_ref[h] = jnp.exp(s1 - v1[0])

    group = 2 if heads % 2 == 0 else 1

    def body(g, carry):
        for r in range(group):
            route_head(g * group + r)
        return carry

    lax.fori_loop(0, heads // group, body, 0)


def _peer_route(hn, wqb, k1, k2, layer):
    T, D = hn.shape
    _, nk, half = k1.shape
    heads = wqb.shape[2] // (2 * half)
    tm = _tile(T, 512)
    ospec = pl.BlockSpec((heads, nk, tm), lambda i: (0, 0, i))
    wspec = lambda a: pl.BlockSpec((1,) + a.shape[1:], lambda i: (layer, 0, 0), pipeline_mode=pl.Buffered(1))
    return pl.pallas_call(
        functools.partial(_route_kernel, heads=heads, half=half),
        grid=(T // tm,),
        in_specs=[pl.BlockSpec((tm, D), lambda i: (i, 0)), wspec(wqb), wspec(k1), wspec(k2)],
        out_specs=[ospec] * 4,
        out_shape=[jax.ShapeDtypeStruct((heads, nk, T), F32)] * 4,
        scratch_shapes=[pltpu.VMEM((tm, wqb.shape[2]), F32)],
        compiler_params=_params(("parallel",)),
        name="peer_route",
    )(hn, wqb, k1, k2)


def _experts_kernel(hnt_ref, u_ref, v_ref, r2_ref, e2_ref, cnt_ref, e1_ref, h1_ref, o_ref, w_ref,
                    *, heads, nk, tm, te):
    j = pl.program_id(1)

    @pl.when(j == 0)
    def _():
        o_ref[...] = h1_ref[...]

    groups = te // nk
    g0 = pl.multiple_of(j * groups, groups)
    na, rc = 4, 64

    for tb in range(tm // LANES):
        lanes = slice(tb * LANES, (tb + 1) * LANES)
        for c in range(nk // rc):
            for ag in range(groups // na):
                w = [jnp.zeros((rc, LANES), F32) for _ in range(na)]
                for h in range(heads):
                    r2 = r2_ref[h, c * rc:(c + 1) * rc, lanes]
                    e2 = e2_ref[h, c * rc:(c + 1) * rc, lanes]
                    cg = cnt_ref[h, pl.ds(g0, groups), lanes]
                    e1g = e1_ref[h, pl.ds(g0, groups), lanes]
                    for a in range(na):
                        i = ag * na + a
                        w[a] = w[a] + jnp.where(r2 < cg[i:i + 1], e2, 0.0) * e1g[i:i + 1]
                for a in range(na):
                    r0 = (ag * na + a) * nk + c * rc
                    w_ref[r0:r0 + rc, lanes] = w[a]

    hid = jnp.dot(u_ref[0], hnt_ref[...], preferred_element_type=F32)
    at = 0.5 * hid * (1.0 + lax.erf(hid * (1.0 / math.sqrt(2.0)))) * w_ref[...]
    o_ref[...] += jnp.dot(at.T.astype(BF16), v_ref[0], preferred_element_type=F32)


def _peer_experts(hnt, u16, v16, r2, e2, cnt, e1, h1, layer):
    D, T = hnt.shape
    E = u16.shape[1]
    heads, nk, _ = r2.shape
    tm, te = _tile(T, 512), _tile(E, 8 * nk)
    assert te == 8 * nk
    once = dict(pipeline_mode=pl.Buffered(1))
    rspec = pl.BlockSpec((heads, nk, tm), lambda i, j: (0, 0, i), **once)
    tspec = pl.BlockSpec((1, te, D), lambda i, j: (layer, j, 0))
    return pl.pallas_call(
        functools.partial(_experts_kernel, heads=heads, nk=nk, tm=tm, te=te),
        grid=(T // tm, E // te),
        in_specs=[pl.BlockSpec((D, tm), lambda i, j: (0, i), **once), tspec, tspec,
                  rspec, rspec, rspec, rspec, pl.BlockSpec((tm, D), lambda i, j: (i, 0), **once)],
        out_specs=pl.BlockSpec((tm, D), lambda i, j: (i, 0)),
        out_shape=jax.ShapeDtypeStruct((T, D), F32),
        scratch_shapes=[pltpu.VMEM((te, tm), F32)],
        compiler_params=_params(("parallel", "arbitrary")),
        name="peer_experts",
    )(hnt, u16, v16, r2, e2, cnt, e1, h1)


def _ple_kernel(h_ref, g_ref, wg_ref, p_ref, wp_ref, gf_ref, o_ref, *, final):
    h = h_ref[...]
    xn = _rms(h, g_ref[...]).astype(BF16)
    gate = jax.nn.sigmoid(jnp.dot(xn, wg_ref[0], preferred_element_type=F32))
    out = h + gate * _bdot(p_ref[0], wp_ref[0])
    o_ref[...] = _rms(out, gf_ref[...]) if final else out


def _ple(h, g, w_gate16, p, w_proj16, layer, final_g=None):
    T, D = h.shape
    pd = p.shape[2]
    tm = _tile(T, 512)
    vec = pl.BlockSpec((1, D), lambda i: (0, 0))
    gf = (g if final_g is None else final_g).reshape(1, D)
    return pl.pallas_call(
        functools.partial(_ple_kernel, final=final_g is not None),
        grid=(T // tm,),
        in_specs=[pl.BlockSpec((tm, D), lambda i: (i, 0)), vec, _layer_spec(w_gate16, layer),
                  pl.BlockSpec((1, tm, pd), lambda i: (layer, i, 0)), _layer_spec(w_proj16, layer), vec],
        out_specs=pl.BlockSpec((tm, D), lambda i: (i, 0)),
        out_shape=jax.ShapeDtypeStruct((T, D), F32),
        compiler_params=_params(("parallel",)),
        name="ple",
    )(h, g.reshape(1, D), w_gate16, p, w_proj16, gf)


def kernel(x, p, norm_mix_g, w_in, pool_w, pool_scale, w_pool_out, short_conv_w, short_conv_b, filt_w1, filt_b1, filt_w2, filt_b2, filt_w3, filt_freq, filt_bias, w_hyena_out, w_o, norm_ffn_g, peer_wq, peer_k1, peer_k2, peer_u, peer_v, norm_ple_g, ple_w_gate, ple_w_proj, final_norm_g):
    B, L, D = x.shape
    T = B * L
    depth = w_in.shape[0]
    pw = pool_scale.shape[-1]
    C = w_hyena_out.shape[1]
    hy_col, gate_col = pw, pw + 3 * C
    fwd, inv = _dft_tables(L, _tile(L, 512))
    pool_w16, wpo16, why16, wo16, wq16, u16, v16, wg16, wp16 = (
        w.astype(BF16) for w in (pool_w, w_pool_out, w_hyena_out, w_o, peer_wq, peer_u, peer_v,
                                 ple_w_gate, ple_w_proj))
    p3 = p.reshape(depth, T, -1)
    h = x.reshape(T, D)
    for i in range(depth):
        proj = _inproj(h, norm_mix_g[i], w_in, i, 2 * D).reshape(B, L, -1)
        ga = _pool_branch(proj, pool_w16, pool_scale[i], wpo16, i, gate_col)
        hs, hd, nq = _hyena_filters(L, filt_w1[i], filt_b1[i], filt_w2[i], filt_b2[i], filt_w3[i], filt_freq[i], C)
        gr, gi = _filter_spectrum(fwd, hs, hd)
        z2 = _hyena_operator(proj, short_conv_w[i], short_conv_b[i], fwd, inv, gr, gi, nq, filt_bias[i], hy_col, C)
        h1, hn, hnt = _merge(ga.reshape(T, D), z2.reshape(T, C), proj.reshape(T, -1), gate_col + D,
                             why16, wo16, i, h, norm_ffn_g[i])
        r2, e2, cnt, e1 = _peer_route(hn, wq16, peer_k1, peer_k2, i)
        h2 = _peer_experts(hnt, u16, v16, r2, e2, cnt, e1, h1, i)
        h = _ple(h2, norm_ple_g[i], wg16, p3, wp16, i, final_norm_g if i == depth - 1 else None)
    return h.reshape(B, L, D)
```

```python
import functools
import math

import jax
import jax.numpy as jnp
from jax import lax
from jax.experimental import pallas as pl
from jax.experimental.pallas import tpu as pltpu

F32 = jnp.float32
BF16 = jnp.bfloat16

NORM_EPS = 1e-6
POOL_WINDOWS = (2, 4, 8, 16)
SHORT_CONV = 3
FILTER_EMB = 33
DECAY_TARGET = 1e-2
DECAY_MIN = math.log(DECAY_TARGET) / 1.5
DECAY_MAX = math.log(DECAY_TARGET) / 0.3
PEER_TOPK = 16
LANES = 128
V7X_VMEM_BYTES = 64 * 1024 * 1024
VMEM_LIMIT = V7X_VMEM_BYTES - 8 * 1024 * 1024
HI = lax.Precision.HIGHEST


def _tile(n, pref):
    return pref if n % pref == 0 else n


def _params(sem, **options):
    return pltpu.CompilerParams(dimension_semantics=sem, vmem_limit_bytes=VMEM_LIMIT, **options)


def _bdot(a, b):
    return jnp.dot(a.astype(BF16), b.astype(BF16), preferred_element_type=F32)


def _rms(x, g):
    ms = jnp.mean(x * x, axis=-1, keepdims=True)
    return x * lax.rsqrt(ms + NORM_EPS) * g


def _inproj_kernel(x_ref, g_ref, w_ref, o_ref, xn_ref, *, n_plain):
    j = pl.program_id(1)

    @pl.when(j == 0)
    def _():
        xn_ref[...] = _rms(x_ref[...], g_ref[...]).astype(BF16)

    acc = jnp.dot(xn_ref[...], w_ref[0].astype(BF16), preferred_element_type=F32)
    o_ref[...] = jnp.where(j >= n_plain, jax.nn.sigmoid(acc), acc).astype(o_ref.dtype)


def _inproj(h, g, w, layer, n_gate_cols):
    T, D = h.shape
    N = w.shape[2]
    tm, tn = _tile(T, 1024), _tile(N, 512)
    assert (N - n_gate_cols) % tn == 0
    return pl.pallas_call(
        functools.partial(_inproj_kernel, n_plain=(N - n_gate_cols) // tn),
        grid=(T // tm, N // tn),
        in_specs=[pl.BlockSpec((tm, D), lambda i, j: (i, 0)),
                  pl.BlockSpec((1, D), lambda i, j: (0, 0)),
                  pl.BlockSpec((1, D, tn), lambda i, j: (layer, 0, j))],
        out_specs=pl.BlockSpec((tm, tn), lambda i, j: (i, j)),
        out_shape=jax.ShapeDtypeStruct((T, N), BF16),
        scratch_shapes=[pltpu.VMEM((tm, D), BF16)],
        compiler_params=_params(("parallel", "arbitrary")),
        name="inproj",
    )(h, g.reshape(1, D), w)


def _pool_kernel(pin_ref, gate_ref, pw_ref, ps_ref, wo_ref, o_ref, *, tm, kw, seq):
    t0 = pl.program_id(1) * tm
    k0 = pl.multiple_of(jnp.clip(t0 - (kw - tm) // 2, 0, seq - kw), LANES)
    r = t0 + lax.broadcasted_iota(jnp.int32, (tm, 1), 0)
    s = k0 + lax.broadcasted_iota(jnp.int32, (1, kw), 1)
    group = pw_ref.shape[2]
    parts = []
    for gi, win in enumerate(POOL_WINDOWS):
        lo = jnp.clip(r - win // 2, 0, seq - 1)
        hi = jnp.clip(r + win - win // 2 - 1, 0, seq - 1)
        inv = 1.0 / (hi - lo + 1).astype(F32)
        band = jnp.where((s >= lo) & (s <= hi), inv, 0.0) - jnp.where(s == r, 1.0, 0.0)
        u = pin_ref[0, pl.ds(k0, kw), gi * group:(gi + 1) * group]
        centred = jnp.dot(band.astype(BF16), u, preferred_element_type=F32)
        parts.append(_bdot(centred, pw_ref[0, gi]))
    pooled = jnp.concatenate(parts, axis=-1) * ps_ref[...]
    a = _bdot(pooled, wo_ref[0])
    o_ref[0] = (a * gate_ref[0].astype(F32)).astype(o_ref.dtype)


def _layer_spec(a, layer):
    nd = a.ndim - 1
    return pl.BlockSpec((1,) + a.shape[1:], lambda *_: (layer,) + (0,) * nd, pipeline_mode=pl.Buffered(1))


def _pool_branch(proj, pool_w16, pool_scale, w_pool_out16, layer, gate_col):
    B, L, _ = proj.shape
    _, pw, D = w_pool_out16.shape
    tm = _tile(L, 256)
    kw = min(L, 2 * tm)
    assert max(POOL_WINDOWS) <= (kw - tm) // 2 or kw == L
    return pl.pallas_call(
        functools.partial(_pool_kernel, tm=tm, kw=kw, seq=L),
        grid=(B, L // tm),
        in_specs=[pl.BlockSpec((1, L, pw), lambda b, i: (b, 0, 0)),
                  pl.BlockSpec((1, tm, D), lambda b, i: (b, i, gate_col // D)),
                  _layer_spec(pool_w16, layer),
                  pl.BlockSpec((1, pw), lambda b, i: (0, 0)),
                  _layer_spec(w_pool_out16, layer)],
        out_specs=pl.BlockSpec((1, tm, D), lambda b, i: (b, i, 0)),
        out_shape=jax.ShapeDtypeStruct((B, L, D), BF16),
        compiler_params=_params(("parallel", "arbitrary")),
        name="pool_branch",
    )(proj, proj, pool_w16, pool_scale.reshape(1, pw), w_pool_out16)


def _filter_kernel(fb_ref, w1t_ref, w1c_ref, w1s_ref, b1_ref, w2_ref, b2_ref, fr_ref, dl_ref,
                   w3a_ref, w3b_ref, w3c_ref, w3d_ref, hs_ref, hd_ref, nq_ref, hdn_ref, *, seq):
    @pl.when(pl.program_id(0) == 0)
    def _():
        ti = lax.broadcasted_iota(jnp.int32, (seq, 1), 0).astype(F32)
        t = ti / (seq - 1.0)
        ang = (2.0 * math.pi / seq) * ti * fb_ref[...]
        z = (t * w1t_ref[...] + jnp.dot(jnp.cos(ang), w1c_ref[...], precision=HI, preferred_element_type=F32)
             - jnp.dot(jnp.sin(ang), w1s_ref[...], precision=HI, preferred_element_type=F32))
        hdn = jnp.sin(fr_ref[...] * (z + b1_ref[...]))
        hdn = jnp.sin(fr_ref[...] * (jnp.dot(hdn, w2_ref[...], precision=HI, preferred_element_type=F32)
                                    + b2_ref[...]))
        hdn_ref[...] = hdn

    ti = lax.broadcasted_iota(jnp.int32, (seq, 1), 0)
    t = ti.astype(F32) / (seq - 1.0)
    decay = jnp.exp(-t * dl_ref[...])
    sign = jnp.where(ti % 2 == 0, 1.0, -1.0)

    def filt(w3_ref):
        f = jnp.dot(hdn_ref[...], w3_ref[...], precision=HI, preferred_element_type=F32) * decay
        return f / (jnp.sum(jnp.abs(f), axis=0, keepdims=True) + 1e-6)

    for order, (wf_ref, wb_ref) in enumerate(((w3a_ref, w3c_ref), (w3b_ref, w3d_ref))):
        hf, hb = filt(wf_ref), filt(wb_ref)
        hsum = hf + hb
        hs_ref[order] = hsum.astype(hs_ref.dtype)
        hd_ref[order] = (hf - hb).astype(hd_ref.dtype)
        nq_ref[order] = jnp.sum(hsum * sign, axis=0, keepdims=True)


def _pad_to(a, shape):
    return jnp.pad(a, [(0, s - d) for d, s in zip(a.shape, shape)])


def _hyena_filters(seq, w1, b1, w2, b2, w3, freq, width):
    hid = w1.shape[1]
    bands = (FILTER_EMB - 1) // 2
    hp = max(LANES, hid)
    fb = _pad_to(jnp.linspace(1e-4, bands - 1, bands, dtype=F32)[None, :], (1, LANES))
    w1t = _pad_to(w1[0:1], (1, hp))
    w1c = _pad_to(w1[1:1 + bands], (LANES, hp))
    w1s = _pad_to(w1[1 + bands:], (LANES, hp))
    b1p, b2p, frp = (_pad_to(v.reshape(1, hid), (1, hp)) for v in (b1, b2, freq))
    w2p = _pad_to(w2, (hp, hp))
    w3p = _pad_to(w3, (hp, w3.shape[1]))
    deltas = jnp.abs(jnp.linspace(DECAY_MIN, DECAY_MAX, width, dtype=F32))[None, :]
    tc = _tile(width, 256)
    nc = width // tc
    small = lambda shape: pl.BlockSpec(shape, lambda c: (0,) * len(shape))
    w3spec = lambda g: pl.BlockSpec((hp, tc), lambda c, g=g: (0, g * nc + c))
    out3 = lambda rows: pl.BlockSpec((2, rows, tc), lambda c: (0, 0, c))
    return pl.pallas_call(
        functools.partial(_filter_kernel, seq=seq),
        grid=(nc,),
        in_specs=[small((1, LANES)), small((1, hp)), small((LANES, hp)), small((LANES, hp)), small((1, hp)),
                  small((hp, hp)), small((1, hp)), small((1, hp)), pl.BlockSpec((1, tc), lambda c: (0, c)),
                  w3spec(0), w3spec(1), w3spec(2), w3spec(3)],
        out_specs=[out3(seq), out3(seq), out3(1)],
        out_shape=[jax.ShapeDtypeStruct((2, seq, width), BF16), jax.ShapeDtypeStruct((2, seq, width), BF16),
                   jax.ShapeDtypeStruct((2, 1, width), F32)],
        scratch_shapes=[pltpu.VMEM((seq, hp), F32)],
        compiler_params=_params(("arbitrary",)),
        name="hyena_filters",
    )(fb, w1t, w1c, w1s, b1p, w2p, b2p, frp, deltas, w3p, w3p, w3p, w3p)


def _dft_tables(seq, fc):
    n = 2 * seq
    f = jnp.arange(seq, dtype=jnp.int32)[:, None]
    t = jnp.arange(seq, dtype=jnp.int32)[None, :]
    ang = ((f * t) & (n - 1)).astype(F32) * (2.0 * math.pi / n)
    cosm = jnp.cos(ang)
    sinm = jnp.where(f == 0, jnp.where(t % 2 == 0, 1.0, -1.0), -jnp.sin(ang))
    nk = seq // fc
    fwd = jnp.concatenate([cosm.reshape(nk, fc, seq), sinm.reshape(nk, fc, seq)], axis=1)
    inv = jnp.transpose(fwd, (0, 2, 1)) * (2.0 / n)
    return fwd.astype(BF16), inv.astype(BF16)


def _spectrum_kernel(w_ref, hs_ref, hd_ref, gr_ref, gi_ref, *, fc):
    gr_ref[0] = jnp.dot(w_ref[0, :fc, :], hs_ref[0], preferred_element_type=F32)
    gi_ref[0] = jnp.dot(w_ref[0, fc:, :], hd_ref[0], preferred_element_type=F32)


def _filter_spectrum(fwd, hs, hd):
    nk, fc2, seq = fwd.shape
    fc = fc2 // 2
    order, _, width = hs.shape
    tc = _tile(width, 512)
    hspec = pl.BlockSpec((1, seq, tc), lambda k, o, c: (o, 0, c))
    gspec = pl.BlockSpec((1, fc, tc), lambda k, o, c: (o, k, c))
    gshape = jax.ShapeDtypeStruct((order, seq, width), F32)
    return pl.pallas_call(
        functools.partial(_spectrum_kernel, fc=fc),
        grid=(nk, order, width // tc),
        in_specs=[pl.BlockSpec((1, fc2, seq), lambda k, o, c: (k, 0, 0)), hspec, hspec],
        out_specs=[gspec, gspec],
        out_shape=[gshape, gshape],
        compiler_params=_params(("arbitrary", "arbitrary", "arbitrary")),
        name="filter_spectrum",
    )(fwd, hs, hd)


def _hyena_kernel(v_ref, x1_ref, x2_ref, cwv_ref, cw1_ref, cw2_ref, cbv_ref, cb1_ref, cb2_ref,
                  fwd_ref, inv_ref, gr_ref, gi_ref, nq_ref, fbias_ref, o_ref,
                  u_ref, ub_ref, acc_ref, *, seq, fc):
    o, k = pl.program_id(2), pl.program_id(3)
    last = pl.num_programs(3) - 1
    row = lax.broadcasted_iota(jnp.int32, (seq, 1), 0)

    def short_conv(x_ref, w_ref, b_ref):
        x = x_ref[0].astype(F32)
        prev = jnp.where(row == 0, 0.0, pltpu.roll(x, 1, 0))
        nxt = jnp.where(row == seq - 1, 0.0, pltpu.roll(x, seq - 1, 0))
        return prev * w_ref[0:1, :] + x * w_ref[1:2, :] + nxt * w_ref[2:3, :] + b_ref[...]

    @pl.when((o == 0) & (k == 0))
    def _():
        v = short_conv(v_ref, cwv_ref, cbv_ref)
        u_ref[...] = v
        ub_ref[...] = v.astype(BF16)

    @pl.when(k == 0)
    def _():
        acc_ref[...] = jnp.zeros_like(acc_ref)

    spec = jnp.dot(fwd_ref[0], ub_ref[...], preferred_element_type=F32)
    vr, vi = spec[:fc], spec[fc:]
    gr, gi = gr_ref[0], gi_ref[0]
    special = (lax.broadcasted_iota(jnp.int32, (fc, 1), 0) == 0) & (k == 0)
    yr = jnp.where(special, 0.5 * vr * gr, vr * gr - vi * gi)
    yi = jnp.where(special, 0.5 * vi * nq_ref[0], vr * gi + vi * gr)
    y = jnp.concatenate([yr, yi], axis=0).astype(BF16)
    acc_ref[...] += jnp.dot(inv_ref[0], y, preferred_element_type=F32)

    @pl.when((k == last) & (o == 0))
    def _():
        z = (acc_ref[...] + u_ref[...] * fbias_ref[0]) * short_conv(x1_ref, cw1_ref, cb1_ref)
        u_ref[...] = z
        ub_ref[...] = z.astype(BF16)

    @pl.when((k == last) & (o == 1))
    def _():
        gate = short_conv(x2_ref, cw2_ref, cb2_ref)
        o_ref[0] = ((acc_ref[...] + u_ref[...] * fbias_ref[0]) * gate).astype(o_ref.dtype)


def _hyena_operator(proj, conv_w, conv_b, fwd, inv, gr, gi, nq, filt_bias, col0, width):
    B, L, _ = proj.shape
    nk, fc2, _ = fwd.shape
    tc = _tile(width, 512)
    nc = width // tc
    assert col0 % tc == 0
    xspec = lambda part: pl.BlockSpec((1, L, tc), lambda b, c, o, k, part=part: (b, 0, col0 // tc + part * nc + c),
                                      pipeline_mode=pl.Buffered(1))
    wspec = lambda rows, part: pl.BlockSpec((rows, tc), lambda b, c, o, k, part=part: (0, part * nc + c))
    ospec = lambda rows: pl.BlockSpec((1, rows, tc), lambda b, c, o, k: (o, 0, c))
    cb = conv_b.reshape(1, -1)
    return pl.pallas_call(
        functools.partial(_hyena_kernel, seq=L, fc=fc2 // 2),
        grid=(B, nc, 2, nk),
        in_specs=[xspec(0), xspec(1), xspec(2), wspec(SHORT_CONV, 0), wspec(SHORT_CONV, 1), wspec(SHORT_CONV, 2),
                  wspec(1, 0), wspec(1, 1), wspec(1, 2),
                  pl.BlockSpec((1, fc2, L), lambda b, c, o, k: (k, 0, 0)),
                  pl.BlockSpec((1, L, fc2), lambda b, c, o, k: (k, 0, 0)),
                  pl.BlockSpec((1, fc2 // 2, tc), lambda b, c, o, k: (o, k, c)),
                  pl.BlockSpec((1, fc2 // 2, tc), lambda b, c, o, k: (o, k, c)),
                  ospec(1), ospec(1)],
        out_specs=pl.BlockSpec((1, L, tc), lambda b, c, o, k: (b, 0, c)),
        out_shape=jax.ShapeDtypeStruct((B, L, width), BF16),
        scratch_shapes=[pltpu.VMEM((L, tc), F32), pltpu.VMEM((L, tc), BF16), pltpu.VMEM((L, tc), F32)],
        compiler_params=_params(("arbitrary",) * 4),
        name="hyena_operator",
    )(proj, proj, proj, conv_w, conv_w, conv_w, cb, cb, cb, fwd, inv, gr, gi, nq,
      filt_bias.reshape(2, 1, width))


def _merge_kernel(ga_ref, z_ref, gate_ref, wh_ref, wo_ref, h_ref, g_ref, h1_ref, hn_ref, hnt_ref):
    hy = jnp.dot(z_ref[...], wh_ref[0], preferred_element_type=F32)
    mix = ga_ref[...].astype(F32) + gate_ref[...].astype(F32) * hy
    h1 = h_ref[...] + _bdot(mix, wo_ref[0])
    h1_ref[...] = h1
    hn = _rms(h1, g_ref[...])
    hn_ref[...] = hn.astype(BF16)
    hnt_ref[...] = hn.T.astype(hnt_ref.dtype)


def _merge(ga, z2, proj, gate_col, whb, wob, layer, h, g):
    T, D = h.shape
    C = z2.shape[1]
    tm = _tile(T, 256)
    row = lambda w: pl.BlockSpec((tm, w), lambda i: (i, 0))
    full = lambda a: pl.BlockSpec(a.shape, lambda i: (0,) * a.ndim)
    g2 = g.reshape(1, D)
    return pl.pallas_call(
        _merge_kernel,
        grid=(T // tm,),
        in_specs=[row(D), row(C), pl.BlockSpec((tm, D), lambda i: (i, gate_col // D)),
                  _layer_spec(whb, layer), _layer_spec(wob, layer), row(D), full(g2)],
        out_specs=[row(D), row(D), pl.BlockSpec((D, tm), lambda i: (0, i))],
        out_shape=[jax.ShapeDtypeStruct((T, D), F32), jax.ShapeDtypeStruct((T, D), BF16),
                   jax.ShapeDtypeStruct((D, T), BF16)],
        compiler_params=_params(("parallel",)),
        name="mixer_merge",
    )(ga, z2, proj, whb, wob, h, g2)


UNRANKED = 127.0


def _extract_top(x, count, with_rank=False):
    out = []
    rank = jnp.full(x.shape, UNRANKED, F32) if with_rank else None
    for k in range(count):
        m = jnp.max(x, axis=0, keepdims=True)
        out.append(m)
        hit = x == m
        if with_rank:
            rank = jnp.where(hit, float(k), rank)
        x = jnp.where(hit, -jnp.inf, x)
    return (out, rank) if with_rank else out


def _route_kernel(hn_ref, wq_ref, k1_ref, k2_ref, r2_ref, e2_ref, cnt_ref, e1_ref, q_ref, *, heads, half):
    q_ref[...] = jnp.dot(hn_ref[...], wq_ref[0], preferred_element_type=F32)
    k1, k2 = k1_ref[0].astype(BF16), k2_ref[0].astype(BF16)
    nt = (((1,), (1,)), ((), ()))
    nbest = PEER_TOPK + 1
    span = [-(-(nbest // (a + 1)) // 8) * 8 for a in range(nbest)]

    def route_head(h):
        base = pl.multiple_of(h * 2 * half, 2 * half)
        q1 = q_ref[:, pl.ds(base, half)].astype(BF16)
        q2 = q_ref[:, pl.ds(base + half, half)].astype(BF16)
        s1 = lax.dot_general(k1, q1, nt, preferred_element_type=F32)
        s2 = lax.dot_general(k2, q2, nt, preferred_element_type=F32)
        v1 = _extract_top(s1, nbest)
        v2, rank2 = _extract_top(s2, nbest, with_rank=True)
        v2p = jnp.concatenate(v2 + [jnp.full_like(v2[0], -jnp.inf)] * (span[0] - nbest), axis=0)
        cand = jnp.concatenate([v1[a] + v2p[:span[a]] for a in range(nbest)], axis=0)
        best = _extract_top(cand, nbest)
        top = best[0]
        z = sum(jnp.exp(b - top) for b in best[1:PEER_TOPK]) + 1.0
        thr = 0.5 * (best[PEER_TOPK - 1] + best[PEER_TOPK])
        need = thr - s1
        r2_ref[h] = rank2.astype(r2_ref.dtype)
        e2_ref[h] = (jnp.exp(s2 - v2[0]) / z).astype(e2_ref.dtype)
        cnt_ref[h] = sum(jnp.where(v >= need, 1.0, 0.0) for v in v2[:PEER_TOPK])
        e1_ref[h] = 0.5 * jnp.exp(s1 - v1[0])

    group = 2 if heads % 2 == 0 else 1

    def body(g, carry):
        for r in range(group):
            route_head(g * group + r)
        return carry

    lax.fori_loop(0, heads // group, body, 0)


def _peer_route(hn, wqb, k1, k2, layer):
    T, D = hn.shape
    _, nk, half = k1.shape
    heads = wqb.shape[2] // (2 * half)
    tm = _tile(T, 512)
    ospec = pl.BlockSpec((heads, nk, tm), lambda i: (0, 0, i))
    wspec = lambda a: pl.BlockSpec((1,) + a.shape[1:], lambda i: (layer, 0, 0), pipeline_mode=pl.Buffered(1))
    return pl.pallas_call(
        functools.partial(_route_kernel, heads=heads, half=half),
        grid=(T // tm,),
        in_specs=[pl.BlockSpec((tm, D), lambda i: (i, 0)), wspec(wqb), wspec(k1), wspec(k2)],
        out_specs=[ospec] * 4,
        out_shape=[jax.ShapeDtypeStruct((heads, nk, T), F32)] * 4,
        scratch_shapes=[pltpu.VMEM((tm, wqb.shape[2]), F32)],
        compiler_params=_params(("parallel",)),
        name="peer_route",
    )(hn, wqb, k1, k2)


def _experts_kernel(hnt_ref, u_ref, v_ref, r2_ref, e2_ref, cnt_ref, e1_ref, o_ref, w_ref,
                    *, heads, nk, tm, te):
    j = pl.program_id(1)

    @pl.when(j == 0)
    def _():
        o_ref[...] = jnp.zeros_like(o_ref)

    groups = te // nk
    g0 = pl.multiple_of(j * groups, groups)
    na, rc = 4, 64

    for tb in range(tm // LANES):
        lanes = slice(tb * LANES, (tb + 1) * LANES)
        for c in range(nk // rc):
            for ag in range(groups // na):
                w = [jnp.zeros((rc, LANES), F32) for _ in range(na)]
                for h in range(heads):
                    r2 = r2_ref[h, c * rc:(c + 1) * rc, lanes]
                    e2 = e2_ref[h, c * rc:(c + 1) * rc, lanes]
                    cg = cnt_ref[h, pl.ds(g0, groups), lanes]
                    e1g = e1_ref[h, pl.ds(g0, groups), lanes]
                    for a in range(na):
                        i = ag * na + a
                        w[a] = jnp.where(r2 < cg[i:i + 1], w[a] + e2 * e1g[i:i + 1], w[a])
                for a in range(na):
                    r0 = (ag * na + a) * nk + c * rc
                    w_ref[r0:r0 + rc, lanes] = w[a]

    nn = (((1,), (0,)), ((), ()))
    hid = lax.dot_general(u_ref[0], hnt_ref[...], nn, preferred_element_type=F32)
    at = hid * (1.0 + lax.erf(hid * (1.0 / math.sqrt(2.0)))) * w_ref[...]
    o_ref[...] += lax.dot_general(at.T.astype(BF16), v_ref[0], nn, preferred_element_type=F32)


def _peer_experts(hnt, u_tab, v_tab, r2, e2, cnt, e1, layer):
    D, T = hnt.shape
    E = u_tab.shape[1]
    heads, nk, _ = r2.shape
    tm, te = _tile(T, 512), _tile(E, 8 * nk)
    assert te == 8 * nk
    once = dict(pipeline_mode=pl.Buffered(1))
    rspec = pl.BlockSpec((heads, nk, tm), lambda i, j: (0, 0, i), **once)
    tspec = pl.BlockSpec((1, te, D), lambda i, j: (layer, j, 0))
    return pl.pallas_call(
        functools.partial(_experts_kernel, heads=heads, nk=nk, tm=tm, te=te),
        grid=(T // tm, E // te),
        in_specs=[pl.BlockSpec((D, tm), lambda i, j: (0, i), **once), tspec, tspec,
                  rspec, rspec, rspec, rspec],
        out_specs=pl.BlockSpec((tm, D), lambda i, j: (i, 0), **once),
        out_shape=jax.ShapeDtypeStruct((T, D), F32),
        scratch_shapes=[pltpu.VMEM((te, tm), F32)],
        compiler_params=_params(("parallel", "arbitrary")),
        name="peer_experts",
    )(hnt, u_tab, v_tab, r2, e2, cnt, e1)


def _ple_kernel(h_ref, d_ref, g_ref, wg_ref, p_ref, wp_ref, gf_ref, o_ref, *, final):
    h = h_ref[...] + d_ref[...]
    xn = _rms(h, g_ref[...]).astype(BF16)
    gate = jax.nn.sigmoid(jnp.dot(xn, wg_ref[0], preferred_element_type=F32))
    out = h + gate * _bdot(p_ref[0], wp_ref[0])
    o_ref[...] = _rms(out, gf_ref[...]) if final else out


def _ple(h, delta, g, w_gate16, p, w_proj16, layer, final_g=None):
    T, D = h.shape
    pd = p.shape[2]
    tm = _tile(T, 512)
    row = pl.BlockSpec((tm, D), lambda i: (i, 0))
    vec = pl.BlockSpec((1, D), lambda i: (0, 0))
    gf = (g if final_g is None else final_g).reshape(1, D)
    return pl.pallas_call(
        functools.partial(_ple_kernel, final=final_g is not None),
        grid=(T // tm,),
        in_specs=[row, row, vec, _layer_spec(w_gate16, layer),
                  pl.BlockSpec((1, tm, pd), lambda i: (layer, i, 0)), _layer_spec(w_proj16, layer), vec],
        out_specs=row,
        out_shape=jax.ShapeDtypeStruct((T, D), F32),
        compiler_params=_params(("parallel",)),
        name="ple",
    )(h, delta, g.reshape(1, D), w_gate16, p, w_proj16, gf)


def kernel(x, p, norm_mix_g, w_in, pool_w, pool_scale, w_pool_out, short_conv_w, short_conv_b, filt_w1, filt_b1, filt_w2, filt_b2, filt_w3, filt_freq, filt_bias, w_hyena_out, w_o, norm_ffn_g, peer_wq, peer_k1, peer_k2, peer_u, peer_v, norm_ple_g, ple_w_gate, ple_w_proj, final_norm_g):
    B, L, D = x.shape
    T = B * L
    depth = w_in.shape[0]
    pw = pool_scale.shape[-1]
    C = w_hyena_out.shape[1]
    hy_col, gate_col = pw, pw + 3 * C
    fwd, inv = _dft_tables(L, _tile(L, 512))
    pool_w16, wpo16, why16, wo16, wq16, wg16, wp16 = (
        w.astype(BF16) for w in (pool_w, w_pool_out, w_hyena_out, w_o, peer_wq, ple_w_gate, ple_w_proj))
    p3 = p.reshape(depth, T, -1)
    h = x.reshape(T, D)
    for i in range(depth):
        proj = _inproj(h, norm_mix_g[i], w_in, i, 2 * D).reshape(B, L, -1)
        ga = _pool_branch(proj, pool_w16, pool_scale[i], wpo16, i, gate_col)
        hs, hd, nq = _hyena_filters(L, filt_w1[i], filt_b1[i], filt_w2[i], filt_b2[i], filt_w3[i], filt_freq[i], C)
        gr, gi = _filter_spectrum(fwd, hs, hd)
        z2 = _hyena_operator(proj, short_conv_w[i], short_conv_b[i], fwd, inv, gr, gi, nq, filt_bias[i], hy_col, C)
        h1, hn, hnt = _merge(ga.reshape(T, D), z2.reshape(T, C), proj.reshape(T, -1), gate_col + D,
                             why16, wo16, i, h, norm_ffn_g[i])
        r2, e2, cnt, e1 = _peer_route(hn, wq16, peer_k1, peer_k2, i)
        po = _peer_experts(hnt, peer_u, peer_v, r2, e2, cnt, e1, i)
        h = _ple(h1, po, norm_ple_g[i], wg16, p3, wp16, i, final_norm_g if i == depth - 1 else None)
    return h.reshape(B, L, D)
```

```python
import functools
import math

import jax
import jax.numpy as jnp
from jax import lax
from jax.experimental import pallas as pl
from jax.experimental.pallas import tpu as pltpu

F32 = jnp.float32
BF16 = jnp.bfloat16

NORM_EPS = 1e-6
POOL_WINDOWS = (2, 4, 8, 16)
SHORT_CONV = 3
FILTER_EMB = 33
DECAY_TARGET = 1e-2
DECAY_MIN = math.log(DECAY_TARGET) / 1.5
DECAY_MAX = math.log(DECAY_TARGET) / 0.3
PEER_TOPK = 16
LANES = 128
V7X_VMEM_BYTES = 64 * 1024 * 1024
VMEM_LIMIT = V7X_VMEM_BYTES - 8 * 1024 * 1024
HI = lax.Precision.HIGHEST


def _tile(n, pref):
    return pref if n % pref == 0 else n


def _params(sem, **options):
    return pltpu.CompilerParams(dimension_semantics=sem, vmem_limit_bytes=VMEM_LIMIT, **options)


def _bdot(a, b):
    return jnp.dot(a.astype(BF16), b.astype(BF16), preferred_element_type=F32)


def _rms(x, g):
    ms = jnp.mean(x * x, axis=-1, keepdims=True)
    return x * lax.rsqrt(ms + NORM_EPS) * g


def _inproj_kernel(x_ref, g_ref, w_ref, o_ref, xn_ref, *, n_plain):
    j = pl.program_id(1)

    @pl.when(j == 0)
    def _():
        xn_ref[...] = _rms(x_ref[...], g_ref[...]).astype(BF16)

    acc = jnp.dot(xn_ref[...], w_ref[0].astype(BF16), preferred_element_type=F32)
    o_ref[...] = jnp.where(j >= n_plain, jax.nn.sigmoid(acc), acc).astype(o_ref.dtype)


def _inproj(h, g, w, layer, n_gate_cols):
    T, D = h.shape
    N = w.shape[2]
    tm = _tile(T, 1024)
    tn = next(t for t in (1024, 512, 256, LANES) if N % t == 0 and (N - n_gate_cols) % t == 0)
    return pl.pallas_call(
        functools.partial(_inproj_kernel, n_plain=(N - n_gate_cols) // tn),
        grid=(T // tm, N // tn),
        in_specs=[pl.BlockSpec((tm, D), lambda i, j: (i, 0)),
                  pl.BlockSpec((1, D), lambda i, j: (0, 0)),
                  pl.BlockSpec((1, D, tn), lambda i, j: (layer, 0, j))],
        out_specs=pl.BlockSpec((tm, tn), lambda i, j: (i, j)),
        out_shape=jax.ShapeDtypeStruct((T, N), BF16),
        scratch_shapes=[pltpu.VMEM((tm, D), BF16)],
        compiler_params=_params(("parallel", "arbitrary")),
        name="inproj",
    )(h, g.reshape(1, D), w)


def _pool_kernel(pin_ref, gate_ref, pw_ref, ps_ref, wo_ref, o_ref, *, tm, kw, seq):
    t0 = pl.program_id(1) * tm
    k0 = pl.multiple_of(jnp.clip(t0 - (kw - tm) // 2, 0, seq - kw), LANES)
    r = t0 + lax.broadcasted_iota(jnp.int32, (tm, 1), 0)
    s = k0 + lax.broadcasted_iota(jnp.int32, (1, kw), 1)
    group = pw_ref.shape[2]
    parts = []
    for gi, win in enumerate(POOL_WINDOWS):
        lo = jnp.clip(r - win // 2, 0, seq - 1)
        hi = jnp.clip(r + win - win // 2 - 1, 0, seq - 1)
        inv = 1.0 / (hi - lo + 1).astype(F32)
        band = jnp.where((s >= lo) & (s <= hi), inv, 0.0) - jnp.where(s == r, 1.0, 0.0)
        u = pin_ref[0, pl.ds(k0, kw), gi * group:(gi + 1) * group]
        centred = jnp.dot(band.astype(BF16), u, preferred_element_type=F32)
        parts.append(_bdot(centred, pw_ref[0, gi]))
    pooled = jnp.concatenate(parts, axis=-1) * ps_ref[...]
    a = _bdot(pooled, wo_ref[0])
    o_ref[0] = (a * gate_ref[0].astype(F32)).astype(o_ref.dtype)


def _layer_spec(a, layer):
    nd = a.ndim - 1
    return pl.BlockSpec((1,) + a.shape[1:], lambda *_: (layer,) + (0,) * nd, pipeline_mode=pl.Buffered(1))


def _pool_branch(proj, pool_w16, pool_scale, w_pool_out16, layer, gate_col):
    B, L, _ = proj.shape
    _, pw, D = w_pool_out16.shape
    tm = _tile(L, 256)
    kw = min(L, 2 * tm)
    assert max(POOL_WINDOWS) <= (kw - tm) // 2 or kw == L
    return pl.pallas_call(
        functools.partial(_pool_kernel, tm=tm, kw=kw, seq=L),
        grid=(B, L // tm),
        in_specs=[pl.BlockSpec((1, L, pw), lambda b, i: (b, 0, 0)),
                  pl.BlockSpec((1, tm, D), lambda b, i: (b, i, gate_col // D)),
                  _layer_spec(pool_w16, layer),
                  pl.BlockSpec((1, pw), lambda b, i: (0, 0)),
                  _layer_spec(w_pool_out16, layer)],
        out_specs=pl.BlockSpec((1, tm, D), lambda b, i: (b, i, 0)),
        out_shape=jax.ShapeDtypeStruct((B, L, D), BF16),
        compiler_params=_params(("parallel", "arbitrary")),
        name="pool_branch",
    )(proj, proj, pool_w16, pool_scale.reshape(1, pw), w_pool_out16)


def _filter_kernel(fb_ref, w1t_ref, w1c_ref, w1s_ref, b1_ref, w2_ref, b2_ref, fr_ref, dl_ref,
                   w3a_ref, w3b_ref, w3c_ref, w3d_ref, hs_ref, hd_ref, nq_ref, hdn_ref, *, seq):
    @pl.when(pl.program_id(0) == 0)
    def _():
        ti = lax.broadcasted_iota(jnp.int32, (seq, 1), 0).astype(F32)
        t = ti / (seq - 1.0)
        ang = (2.0 * math.pi / seq) * ti * fb_ref[...]
        z = (t * w1t_ref[...] + jnp.dot(jnp.cos(ang), w1c_ref[...], precision=HI, preferred_element_type=F32)
             - jnp.dot(jnp.sin(ang), w1s_ref[...], precision=HI, preferred_element_type=F32))
        hdn = jnp.sin(fr_ref[...] * (z + b1_ref[...]))
        hdn = jnp.sin(fr_ref[...] * (jnp.dot(hdn, w2_ref[...], precision=HI, preferred_element_type=F32)
                                    + b2_ref[...]))
        hdn_ref[...] = hdn

    ti = lax.broadcasted_iota(jnp.int32, (seq, 1), 0)
    t = ti.astype(F32) / (seq - 1.0)
    decay = jnp.exp(-t * dl_ref[...])
    sign = jnp.where(ti % 2 == 0, 1.0, -1.0)

    def filt(w3_ref):
        f = jnp.dot(hdn_ref[...], w3_ref[...], precision=HI, preferred_element_type=F32) * decay
        return f / (jnp.sum(jnp.abs(f), axis=0, keepdims=True) + 1e-6)

    for order, (wf_ref, wb_ref) in enumerate(((w3a_ref, w3c_ref), (w3b_ref, w3d_ref))):
        hf, hb = filt(wf_ref), filt(wb_ref)
        hsum = hf + hb
        hs_ref[order] = hsum.astype(hs_ref.dtype)
        hd_ref[order] = (hf - hb).astype(hd_ref.dtype)
        nq_ref[order] = jnp.sum(hsum * sign, axis=0, keepdims=True)


def _pad_to(a, shape):
    return jnp.pad(a, [(0, s - d) for d, s in zip(a.shape, shape)])


def _hyena_filters(seq, w1, b1, w2, b2, w3, freq, width):
    hid = w1.shape[1]
    bands = (FILTER_EMB - 1) // 2
    hp = max(LANES, hid)
    fb = _pad_to(jnp.linspace(1e-4, bands - 1, bands, dtype=F32)[None, :], (1, LANES))
    w1t = _pad_to(w1[0:1], (1, hp))
    w1c = _pad_to(w1[1:1 + bands], (LANES, hp))
    w1s = _pad_to(w1[1 + bands:], (LANES, hp))
    b1p, b2p, frp = (_pad_to(v.reshape(1, hid), (1, hp)) for v in (b1, b2, freq))
    w2p = _pad_to(w2, (hp, hp))
    w3p = _pad_to(w3, (hp, w3.shape[1]))
    deltas = jnp.abs(jnp.linspace(DECAY_MIN, DECAY_MAX, width, dtype=F32))[None, :]
    tc = _tile(width, 256)
    nc = width // tc
    small = lambda shape: pl.BlockSpec(shape, lambda c: (0,) * len(shape))
    w3spec = lambda g: pl.BlockSpec((hp, tc), lambda c, g=g: (0, g * nc + c))
    out3 = lambda rows: pl.BlockSpec((2, rows, tc), lambda c: (0, 0, c))
    return pl.pallas_call(
        functools.partial(_filter_kernel, seq=seq),
        grid=(nc,),
        in_specs=[small((1, LANES)), small((1, hp)), small((LANES, hp)), small((LANES, hp)), small((1, hp)),
                  small((hp, hp)), small((1, hp)), small((1, hp)), pl.BlockSpec((1, tc), lambda c: (0, c)),
                  w3spec(0), w3spec(1), w3spec(2), w3spec(3)],
        out_specs=[out3(seq), out3(seq), out3(1)],
        out_shape=[jax.ShapeDtypeStruct((2, seq, width), BF16), jax.ShapeDtypeStruct((2, seq, width), BF16),
                   jax.ShapeDtypeStruct((2, 1, width), F32)],
        scratch_shapes=[pltpu.VMEM((seq, hp), F32)],
        compiler_params=_params(("arbitrary",)),
        name="hyena_filters",
    )(fb, w1t, w1c, w1s, b1p, w2p, b2p, frp, deltas, w3p, w3p, w3p, w3p)


def _dft_tables(seq, fc):
    n = 2 * seq
    f = jnp.arange(seq, dtype=jnp.int32)[:, None]
    t = jnp.arange(seq, dtype=jnp.int32)[None, :]
    ang = ((f * t) & (n - 1)).astype(F32) * (2.0 * math.pi / n)
    cosm = jnp.cos(ang)
    sinm = jnp.where(f == 0, jnp.where(t % 2 == 0, 1.0, -1.0), -jnp.sin(ang))
    nk = seq // fc
    fwd = jnp.concatenate([cosm.reshape(nk, fc, seq), sinm.reshape(nk, fc, seq)], axis=1)
    inv = jnp.transpose(fwd, (0, 2, 1)) * (2.0 / n)
    return fwd.astype(BF16), inv.astype(BF16)


def _spectrum_kernel(w_ref, hs_ref, hd_ref, gr_ref, gi_ref, *, fc):
    gr_ref[0] = jnp.dot(w_ref[0, :fc, :], hs_ref[0], preferred_element_type=F32)
    gi_ref[0] = jnp.dot(w_ref[0, fc:, :], hd_ref[0], preferred_element_type=F32)


def _filter_spectrum(fwd, hs, hd):
    nk, fc2, seq = fwd.shape
    fc = fc2 // 2
    order, _, width = hs.shape
    tc = _tile(width, 512)
    hspec = pl.BlockSpec((1, seq, tc), lambda k, o, c: (o, 0, c))
    gspec = pl.BlockSpec((1, fc, tc), lambda k, o, c: (o, k, c))
    gshape = jax.ShapeDtypeStruct((order, seq, width), F32)
    return pl.pallas_call(
        functools.partial(_spectrum_kernel, fc=fc),
        grid=(nk, order, width // tc),
        in_specs=[pl.BlockSpec((1, fc2, seq), lambda k, o, c: (k, 0, 0)), hspec, hspec],
        out_specs=[gspec, gspec],
        out_shape=[gshape, gshape],
        compiler_params=_params(("arbitrary", "arbitrary", "arbitrary")),
        name="filter_spectrum",
    )(fwd, hs, hd)


def _hyena_kernel(v_ref, x1_ref, x2_ref, cwv_ref, cw1_ref, cw2_ref, cbv_ref, cb1_ref, cb2_ref,
                  fwd_ref, inv_ref, gr_ref, gi_ref, nq_ref, fbias_ref, o_ref,
                  u_ref, ub_ref, acc_ref, *, seq, fc):
    o, k = pl.program_id(2), pl.program_id(3)
    last = pl.num_programs(3) - 1
    row = lax.broadcasted_iota(jnp.int32, (seq, 1), 0)

    def short_conv(x_ref, w_ref, b_ref):
        x = x_ref[0].astype(F32)
        prev = jnp.where(row == 0, 0.0, pltpu.roll(x, 1, 0))
        nxt = jnp.where(row == seq - 1, 0.0, pltpu.roll(x, seq - 1, 0))
        return prev * w_ref[0:1, :] + x * w_ref[1:2, :] + nxt * w_ref[2:3, :] + b_ref[...]

    @pl.when((o == 0) & (k == 0))
    def _():
        v = short_conv(v_ref, cwv_ref, cbv_ref)
        u_ref[...] = v
        ub_ref[...] = v.astype(BF16)

    @pl.when(k == 0)
    def _():
        acc_ref[...] = jnp.zeros_like(acc_ref)

    spec = jnp.dot(fwd_ref[0], ub_ref[...], preferred_element_type=F32)
    vr, vi = spec[:fc], spec[fc:]
    gr, gi = gr_ref[0], gi_ref[0]
    special = (lax.broadcasted_iota(jnp.int32, (fc, 1), 0) == 0) & (k == 0)
    yr = jnp.where(special, 0.5 * vr * gr, vr * gr - vi * gi)
    yi = jnp.where(special, 0.5 * vi * nq_ref[0], vr * gi + vi * gr)
    y = jnp.concatenate([yr, yi], axis=0).astype(BF16)
    acc_ref[...] += jnp.dot(inv_ref[0], y, preferred_element_type=F32)

    @pl.when((k == last) & (o == 0))
    def _():
        z = (acc_ref[...] + u_ref[...] * fbias_ref[0]) * short_conv(x1_ref, cw1_ref, cb1_ref)
        u_ref[...] = z
        ub_ref[...] = z.astype(BF16)

    @pl.when((k == last) & (o == 1))
    def _():
        gate = short_conv(x2_ref, cw2_ref, cb2_ref)
        o_ref[0] = ((acc_ref[...] + u_ref[...] * fbias_ref[0]) * gate).astype(o_ref.dtype)


def _hyena_operator(proj, conv_w, conv_b, fwd, inv, gr, gi, nq, filt_bias, col0, width):
    B, L, _ = proj.shape
    nk, fc2, _ = fwd.shape
    tc = _tile(width, 512)
    nc = width // tc
    assert col0 % tc == 0
    xspec = lambda part: pl.BlockSpec((1, L, tc), lambda b, c, o, k, part=part: (b, 0, col0 // tc + part * nc + c),
                                      pipeline_mode=pl.Buffered(1))
    wspec = lambda rows, part: pl.BlockSpec((rows, tc), lambda b, c, o, k, part=part: (0, part * nc + c))
    ospec = lambda rows: pl.BlockSpec((1, rows, tc), lambda b, c, o, k: (o, 0, c))
    cb = conv_b.reshape(1, -1)
    return pl.pallas_call(
        functools.partial(_hyena_kernel, seq=L, fc=fc2 // 2),
        grid=(B, nc, 2, nk),
        in_specs=[xspec(0), xspec(1), xspec(2), wspec(SHORT_CONV, 0), wspec(SHORT_CONV, 1), wspec(SHORT_CONV, 2),
                  wspec(1, 0), wspec(1, 1), wspec(1, 2),
                  pl.BlockSpec((1, fc2, L), lambda b, c, o, k: (k, 0, 0)),
                  pl.BlockSpec((1, L, fc2), lambda b, c, o, k: (k, 0, 0)),
                  pl.BlockSpec((1, fc2 // 2, tc), lambda b, c, o, k: (o, k, c)),
                  pl.BlockSpec((1, fc2 // 2, tc), lambda b, c, o, k: (o, k, c)),
                  ospec(1), ospec(1)],
        out_specs=pl.BlockSpec((1, L, tc), lambda b, c, o, k: (b, 0, c)),
        out_shape=jax.ShapeDtypeStruct((B, L, width), BF16),
        scratch_shapes=[pltpu.VMEM((L, tc), F32), pltpu.VMEM((L, tc), BF16), pltpu.VMEM((L, tc), F32)],
        compiler_params=_params(("arbitrary",) * 4),
        name="hyena_operator",
    )(proj, proj, proj, conv_w, conv_w, conv_w, cb, cb, cb, fwd, inv, gr, gi, nq,
      filt_bias.reshape(2, 1, width))


def _merge_kernel(ga_ref, z_ref, gate_ref, wh_ref, wo_ref, h_ref, g_ref, h1_ref, hn_ref, hnt_ref):
    hy = jnp.dot(z_ref[...], wh_ref[0], preferred_element_type=F32)
    mix = ga_ref[...].astype(F32) + gate_ref[...].astype(F32) * hy
    h1 = h_ref[...] + _bdot(mix, wo_ref[0])
    h1_ref[...] = h1
    hn = _rms(h1, g_ref[...])
    hn_ref[...] = hn.astype(BF16)
    hnt_ref[...] = hn.T.astype(hnt_ref.dtype)


def _merge(ga, z2, proj, gate_col, whb, wob, layer, h, g):
    T, D = h.shape
    C = z2.shape[1]
    tm = _tile(T, 256)
    row = lambda w: pl.BlockSpec((tm, w), lambda i: (i, 0))
    full = lambda a: pl.BlockSpec(a.shape, lambda i: (0,) * a.ndim)
    g2 = g.reshape(1, D)
    return pl.pallas_call(
        _merge_kernel,
        grid=(T // tm,),
        in_specs=[row(D), row(C), pl.BlockSpec((tm, D), lambda i: (i, gate_col // D)),
                  _layer_spec(whb, layer), _layer_spec(wob, layer), row(D), full(g2)],
        out_specs=[row(D), row(D), pl.BlockSpec((D, tm), lambda i: (0, i))],
        out_shape=[jax.ShapeDtypeStruct((T, D), F32), jax.ShapeDtypeStruct((T, D), BF16),
                   jax.ShapeDtypeStruct((D, T), BF16)],
        compiler_params=_params(("parallel",)),
        name="mixer_merge",
    )(ga, z2, proj, whb, wob, h, g2)


SUBLANES = 8


def _sort_network(n):
    p = 1
    while p < n:
        p *= 2
    comps = []

    def merge(lo, cnt, r):
        step = 2 * r
        if step < cnt:
            merge(lo, cnt, step)
            merge(lo + r, cnt, step)
            comps.extend((i, i + r) for i in range(lo + r, lo + cnt - r, step))
        else:
            comps.append((lo, lo + r))

    def sort(lo, cnt):
        if cnt > 1:
            sort(lo, cnt // 2)
            sort(lo + cnt // 2, cnt // 2)
            merge(lo, cnt, 1)

    sort(0, p)
    return [(i, j) for i, j in comps if j < n]


def _top_sorted(x, count):
    lists = [x[SUBLANES * j:SUBLANES * (j + 1)] for j in range(x.shape[0] // SUBLANES)]
    for i, j in _sort_network(len(lists)):
        lists[i], lists[j] = jnp.maximum(lists[i], lists[j]), jnp.minimum(lists[i], lists[j])
    out = []
    for k in range(count):
        m = jnp.max(lists[0], axis=0, keepdims=True)
        out.append(m)
        hit = lists[0] == m
        for j in range(min(count - k - 1, len(lists))):
            below = lists[j + 1] if j + 1 < len(lists) else -jnp.inf
            lists[j] = jnp.where(hit, below, lists[j])
    return out


def _route_kernel(hn_ref, wq_ref, k1_ref, k2_ref, s2_ref, e2_ref, d_ref, e1_ref, q_ref, *, heads, half):
    q_ref[...] = jnp.dot(hn_ref[...], wq_ref[0], preferred_element_type=F32)
    k1, k2 = k1_ref[0].astype(BF16), k2_ref[0].astype(BF16)
    nt = (((1,), (1,)), ((), ()))
    nbest = PEER_TOPK + 1
    padded = -(-nbest // SUBLANES) * SUBLANES
    assert nbest // 2 <= SUBLANES <= padded - SUBLANES
    tm = hn_ref.shape[0]

    def route_block(h, s1, s2, lanes):
        v1, v2 = _top_sorted(s1, nbest), _top_sorted(s2, nbest)
        pad = [jnp.full_like(v2[0], -jnp.inf)] * (padded - nbest)
        v1p, v2p = jnp.concatenate(v1 + pad, axis=0), jnp.concatenate(v2 + pad, axis=0)
        cand = jnp.concatenate([v1[0] + v2p, v1p[SUBLANES:] + v2[0]]
                               + [v1[a] + v2p[:SUBLANES] for a in range(1, SUBLANES)], axis=0)
        best = _top_sorted(cand, nbest)
        z = sum(jnp.exp(b - best[0]) for b in best[1:PEER_TOPK]) + 1.0
        thr = 0.5 * (best[PEER_TOPK - 1] + best[PEER_TOPK])
        s2_ref[h, :, lanes] = s2
        e2_ref[h, :, lanes] = jnp.exp(s2 - v2[0]) / z
        d_ref[h, :, lanes] = thr - s1
        e1_ref[h, :, lanes] = 0.5 * jnp.exp(s1 - v1[0])

    def route_head(h):
        base = pl.multiple_of(h * 2 * half, 2 * half)
        q1 = q_ref[:, pl.ds(base, half)].astype(BF16)
        q2 = q_ref[:, pl.ds(base + half, half)].astype(BF16)
        s1 = lax.dot_general(k1, q1, nt, preferred_element_type=F32)
        s2 = lax.dot_general(k2, q2, nt, preferred_element_type=F32)
        for tb in range(tm // LANES):
            lanes = slice(tb * LANES, (tb + 1) * LANES)
            route_block(h, s1[:, lanes], s2[:, lanes], lanes)

    group = 2 if heads % 2 == 0 else 1

    def body(g, carry):
        for r in range(group):
            route_head(g * group + r)
        return carry

    lax.fori_loop(0, heads // group, body, 0)


def _peer_route(hn, wqb, k1, k2, layer):
    T, D = hn.shape
    _, nk, half = k1.shape
    heads = wqb.shape[2] // (2 * half)
    tm = _tile(T, 512)
    ospec = pl.BlockSpec((heads, nk, tm), lambda i: (0, 0, i))
    wspec = lambda a: pl.BlockSpec((1,) + a.shape[1:], lambda i: (layer, 0, 0), pipeline_mode=pl.Buffered(1))
    return pl.pallas_call(
        functools.partial(_route_kernel, heads=heads, half=half),
        grid=(T // tm,),
        in_specs=[pl.BlockSpec((tm, D), lambda i: (i, 0)), wspec(wqb), wspec(k1), wspec(k2)],
        out_specs=[ospec] * 4,
        out_shape=[jax.ShapeDtypeStruct((heads, nk, T), F32)] * 4,
        scratch_shapes=[pltpu.VMEM((tm, wqb.shape[2]), F32)],
        compiler_params=_params(("parallel",)),
        name="peer_route",
    )(hn, wqb, k1, k2)


def _experts_kernel(hnt_ref, u_ref, v_ref, s2_ref, e2_ref, d_ref, e1_ref, o_ref, w_ref,
                    *, heads, nk, tm, te):
    j = pl.program_id(1)

    @pl.when(j == 0)
    def _():
        o_ref[...] = jnp.zeros_like(o_ref)

    groups = te // nk
    g0 = pl.multiple_of(j * groups, groups)
    na, rc = 4, 64

    for tb in range(tm // LANES):
        lanes = slice(tb * LANES, (tb + 1) * LANES)
        for c in range(nk // rc):
            for ag in range(groups // na):
                w = [jnp.zeros((rc, LANES), F32) for _ in range(na)]
                for h in range(heads):
                    s2 = s2_ref[h, c * rc:(c + 1) * rc, lanes]
                    e2 = e2_ref[h, c * rc:(c + 1) * rc, lanes]
                    dg = d_ref[h, pl.ds(g0, groups), lanes]
                    e1g = e1_ref[h, pl.ds(g0, groups), lanes]
                    for a in range(na):
                        i = ag * na + a
                        w[a] = jnp.where(s2 >= dg[i:i + 1], w[a] + e2 * e1g[i:i + 1], w[a])
                for a in range(na):
                    r0 = (ag * na + a) * nk + c * rc
                    w_ref[r0:r0 + rc, lanes] = w[a]

    nn = (((1,), (0,)), ((), ()))
    hid = lax.dot_general(u_ref[0], hnt_ref[...], nn, preferred_element_type=F32)
    at = hid * (1.0 + lax.erf(hid * (1.0 / math.sqrt(2.0)))) * w_ref[...]
    o_ref[...] += lax.dot_general(at.T.astype(BF16), v_ref[0], nn, preferred_element_type=F32)


def _peer_experts(hnt, u_tab, v_tab, s2, e2, d, e1, layer):
    D, T = hnt.shape
    E = u_tab.shape[1]
    heads, nk, _ = s2.shape
    tm, te = _tile(T, 512), _tile(E, 8 * nk)
    assert te == 8 * nk
    once = dict(pipeline_mode=pl.Buffered(1))
    rspec = pl.BlockSpec((heads, nk, tm), lambda i, j: (0, 0, i), **once)
    tspec = pl.BlockSpec((1, te, D), lambda i, j: (layer, j, 0))
    return pl.pallas_call(
        functools.partial(_experts_kernel, heads=heads, nk=nk, tm=tm, te=te),
        grid=(T // tm, E // te),
        in_specs=[pl.BlockSpec((D, tm), lambda i, j: (0, i), **once), tspec, tspec,
                  rspec, rspec, rspec, rspec],
        out_specs=pl.BlockSpec((tm, D), lambda i, j: (i, 0), **once),
        out_shape=jax.ShapeDtypeStruct((T, D), F32),
        scratch_shapes=[pltpu.VMEM((te, tm), F32)],
        compiler_params=_params(("parallel", "arbitrary")),
        name="peer_experts",
    )(hnt, u_tab, v_tab, s2, e2, d, e1)


def _ple_kernel(h_ref, d_ref, g_ref, wg_ref, p_ref, wp_ref, gf_ref, o_ref, *, final):
    h = h_ref[...] + d_ref[...]
    xn = _rms(h, g_ref[...]).astype(BF16)
    gate = jax.nn.sigmoid(jnp.dot(xn, wg_ref[0], preferred_element_type=F32))
    out = h + gate * _bdot(p_ref[0], wp_ref[0])
    o_ref[...] = _rms(out, gf_ref[...]) if final else out


def _ple(h, delta, g, w_gate16, p, w_proj16, layer, final_g=None):
    T, D = h.shape
    pd = p.shape[2]
    tm = _tile(T, 512)
    row = pl.BlockSpec((tm, D), lambda i: (i, 0))
    vec = pl.BlockSpec((1, D), lambda i: (0, 0))
    gf = (g if final_g is None else final_g).reshape(1, D)
    return pl.pallas_call(
        functools.partial(_ple_kernel, final=final_g is not None),
        grid=(T // tm,),
        in_specs=[row, row, vec, _layer_spec(w_gate16, layer),
                  pl.BlockSpec((1, tm, pd), lambda i: (layer, i, 0)), _layer_spec(w_proj16, layer), vec],
        out_specs=row,
        out_shape=jax.ShapeDtypeStruct((T, D), F32),
        compiler_params=_params(("parallel",)),
        name="ple",
    )(h, delta, g.reshape(1, D), w_gate16, p, w_proj16, gf)


def kernel(x, p, norm_mix_g, w_in, pool_w, pool_scale, w_pool_out, short_conv_w, short_conv_b, filt_w1, filt_b1, filt_w2, filt_b2, filt_w3, filt_freq, filt_bias, w_hyena_out, w_o, norm_ffn_g, peer_wq, peer_k1, peer_k2, peer_u, peer_v, norm_ple_g, ple_w_gate, ple_w_proj, final_norm_g):
    B, L, D = x.shape
    T = B * L
    depth = w_in.shape[0]
    pw = pool_scale.shape[-1]
    C = w_hyena_out.shape[1]
    hy_col, gate_col = pw, pw + 3 * C
    fwd, inv = _dft_tables(L, _tile(L, 512))
    pool_w16, wpo16, why16, wo16, wq16, wg16, wp16 = (
        w.astype(BF16) for w in (pool_w, w_pool_out, w_hyena_out, w_o, peer_wq, ple_w_gate, ple_w_proj))
    p3 = p.reshape(depth, T, -1)
    h = x.reshape(T, D)
    for i in range(depth):
        proj = _inproj(h, norm_mix_g[i], w_in, i, 2 * D).reshape(B, L, -1)
        ga = _pool_branch(proj, pool_w16, pool_scale[i], wpo16, i, gate_col)
        hs, hd, nq = _hyena_filters(L, filt_w1[i], filt_b1[i], filt_w2[i], filt_b2[i], filt_w3[i], filt_freq[i], C)
        gr, gi = _filter_spectrum(fwd, hs, hd)
        z2 = _hyena_operator(proj, short_conv_w[i], short_conv_b[i], fwd, inv, gr, gi, nq, filt_bias[i], hy_col, C)
        h1, hn, hnt = _merge(ga.reshape(T, D), z2.reshape(T, C), proj.reshape(T, -1), gate_col + D,
                             why16, wo16, i, h, norm_ffn_g[i])
        s2, e2, d, e1 = _peer_route(hn, wq16, peer_k1, peer_k2, i)
        po = _peer_experts(hnt, peer_u, peer_v, s2, e2, d, e1, i)
        h = _ple(h1, po, norm_ple_g[i], wg16, p3, wp16, i, final_norm_g if i == depth - 1 else None)
    return h.reshape(B, L, D)
```

```python
import functools
import math

import jax
import jax.numpy as jnp
from jax import lax
from jax.experimental import pallas as pl
from jax.experimental.pallas import tpu as pltpu

F32 = jnp.float32
BF16 = jnp.bfloat16

NORM_EPS = 1e-6
POOL_WINDOWS = (2, 4, 8, 16)
SHORT_CONV = 3
FILTER_EMB = 33
DECAY_TARGET = 1e-2
DECAY_MIN = math.log(DECAY_TARGET) / 1.5
DECAY_MAX = math.log(DECAY_TARGET) / 0.3
PEER_TOPK = 16
LANES = 128
V7X_VMEM_BYTES = 64 * 1024 * 1024
VMEM_LIMIT = V7X_VMEM_BYTES - 8 * 1024 * 1024
HI = lax.Precision.HIGHEST


def _tile(n, pref):
    return pref if n % pref == 0 else n


def _params(sem, **options):
    return pltpu.CompilerParams(dimension_semantics=sem, vmem_limit_bytes=VMEM_LIMIT, **options)


def _bdot(a, b):
    return jnp.dot(a.astype(BF16), b.astype(BF16), preferred_element_type=F32)


def _rms(x, g):
    ms = jnp.mean(x * x, axis=-1, keepdims=True)
    return x * lax.rsqrt(ms + NORM_EPS) * g


def _inproj_kernel(x_ref, g_ref, w_ref, o_ref, xn_ref, *, n_plain):
    j = pl.program_id(1)

    @pl.when(j == 0)
    def _():
        xn_ref[...] = _rms(x_ref[...], g_ref[...]).astype(BF16)

    acc = jnp.dot(xn_ref[...], w_ref[0].astype(BF16), preferred_element_type=F32)
    o_ref[...] = jnp.where(j >= n_plain, jax.nn.sigmoid(acc), acc).astype(o_ref.dtype)


def _inproj(h, g, w, layer, n_gate_cols):
    T, D = h.shape
    N = w.shape[2]
    tm = _tile(T, 1024)
    tn = next(t for t in (1024, 512, 256, LANES) if N % t == 0 and (N - n_gate_cols) % t == 0)
    return pl.pallas_call(
        functools.partial(_inproj_kernel, n_plain=(N - n_gate_cols) // tn),
        grid=(T // tm, N // tn),
        in_specs=[pl.BlockSpec((tm, D), lambda i, j: (i, 0)),
                  pl.BlockSpec((1, D), lambda i, j: (0, 0)),
                  pl.BlockSpec((1, D, tn), lambda i, j: (layer, 0, j))],
        out_specs=pl.BlockSpec((tm, tn), lambda i, j: (i, j)),
        out_shape=jax.ShapeDtypeStruct((T, N), BF16),
        scratch_shapes=[pltpu.VMEM((tm, D), BF16)],
        compiler_params=_params(("parallel", "arbitrary")),
        name="inproj",
    )(h, g.reshape(1, D), w)


def _pool_kernel(pin_ref, gate_ref, pw_ref, ps_ref, wo_ref, o_ref, *, tm, kw, seq):
    t0 = pl.program_id(1) * tm
    k0 = pl.multiple_of(jnp.clip(t0 - (kw - tm) // 2, 0, seq - kw), LANES)
    r = t0 + lax.broadcasted_iota(jnp.int32, (tm, 1), 0)
    s = k0 + lax.broadcasted_iota(jnp.int32, (1, kw), 1)
    group = pw_ref.shape[2]
    parts = []
    for gi, win in enumerate(POOL_WINDOWS):
        lo = jnp.clip(r - win // 2, 0, seq - 1)
        hi = jnp.clip(r + win - win // 2 - 1, 0, seq - 1)
        inv = 1.0 / (hi - lo + 1).astype(F32)
        band = jnp.where((s >= lo) & (s <= hi), inv, 0.0) - jnp.where(s == r, 1.0, 0.0)
        u = pin_ref[0, pl.ds(k0, kw), gi * group:(gi + 1) * group]
        centred = jnp.dot(band.astype(BF16), u, preferred_element_type=F32)
        parts.append(_bdot(centred, pw_ref[0, gi]))
    pooled = jnp.concatenate(parts, axis=-1) * ps_ref[...]
    a = _bdot(pooled, wo_ref[0])
    o_ref[0] = (a * gate_ref[0].astype(F32)).astype(o_ref.dtype)


def _layer_spec(a, layer):
    nd = a.ndim - 1
    return pl.BlockSpec((1,) + a.shape[1:], lambda *_: (layer,) + (0,) * nd, pipeline_mode=pl.Buffered(1))


def _pool_branch(proj, pool_w16, pool_scale, w_pool_out16, layer, gate_col):
    B, L, _ = proj.shape
    _, pw, D = w_pool_out16.shape
    tm = _tile(L, 256)
    kw = min(L, 2 * tm)
    assert max(POOL_WINDOWS) <= (kw - tm) // 2 or kw == L
    return pl.pallas_call(
        functools.partial(_pool_kernel, tm=tm, kw=kw, seq=L),
        grid=(B, L // tm),
        in_specs=[pl.BlockSpec((1, L, pw), lambda b, i: (b, 0, 0)),
                  pl.BlockSpec((1, tm, D), lambda b, i: (b, i, gate_col // D)),
                  _layer_spec(pool_w16, layer),
                  pl.BlockSpec((1, pw), lambda b, i: (0, 0)),
                  _layer_spec(w_pool_out16, layer)],
        out_specs=pl.BlockSpec((1, tm, D), lambda b, i: (b, i, 0)),
        out_shape=jax.ShapeDtypeStruct((B, L, D), BF16),
        compiler_params=_params(("parallel", "arbitrary")),
        name="pool_branch",
    )(proj, proj, pool_w16, pool_scale.reshape(1, pw), w_pool_out16)


def _filter_kernel(fb_ref, w1t_ref, w1c_ref, w1s_ref, b1_ref, w2_ref, b2_ref, fr_ref, dl_ref,
                   w3a_ref, w3b_ref, w3c_ref, w3d_ref, hs_ref, hd_ref, nq_ref, hdn_ref, *, seq):
    @pl.when(pl.program_id(0) == 0)
    def _():
        ti = lax.broadcasted_iota(jnp.int32, (seq, 1), 0).astype(F32)
        t = ti / (seq - 1.0)
        ang = (2.0 * math.pi / seq) * ti * fb_ref[...]
        z = (t * w1t_ref[...] + jnp.dot(jnp.cos(ang), w1c_ref[...], precision=HI, preferred_element_type=F32)
             - jnp.dot(jnp.sin(ang), w1s_ref[...], precision=HI, preferred_element_type=F32))
        hdn = jnp.sin(fr_ref[...] * (z + b1_ref[...]))
        hdn = jnp.sin(fr_ref[...] * (jnp.dot(hdn, w2_ref[...], precision=HI, preferred_element_type=F32)
                                    + b2_ref[...]))
        hdn_ref[...] = hdn

    ti = lax.broadcasted_iota(jnp.int32, (seq, 1), 0)
    t = ti.astype(F32) / (seq - 1.0)
    decay = jnp.exp(-t * dl_ref[...])
    sign = jnp.where(ti % 2 == 0, 1.0, -1.0)

    def filt(w3_ref):
        f = jnp.dot(hdn_ref[...], w3_ref[...], precision=HI, preferred_element_type=F32) * decay
        return f / (jnp.sum(jnp.abs(f), axis=0, keepdims=True) + 1e-6)

    for order, (wf_ref, wb_ref) in enumerate(((w3a_ref, w3c_ref), (w3b_ref, w3d_ref))):
        hf, hb = filt(wf_ref), filt(wb_ref)
        hsum = hf + hb
        hs_ref[order] = hsum.astype(hs_ref.dtype)
        hd_ref[order] = (hf - hb).astype(hd_ref.dtype)
        nq_ref[order] = jnp.sum(hsum * sign, axis=0, keepdims=True)


def _pad_to(a, shape):
    return jnp.pad(a, [(0, s - d) for d, s in zip(a.shape, shape)])


def _hyena_filters(seq, w1, b1, w2, b2, w3, freq, width):
    hid = w1.shape[1]
    bands = (FILTER_EMB - 1) // 2
    hp = max(LANES, hid)
    fb = _pad_to(jnp.linspace(1e-4, bands - 1, bands, dtype=F32)[None, :], (1, LANES))
    w1t = _pad_to(w1[0:1], (1, hp))
    w1c = _pad_to(w1[1:1 + bands], (LANES, hp))
    w1s = _pad_to(w1[1 + bands:], (LANES, hp))
    b1p, b2p, frp = (_pad_to(v.reshape(1, hid), (1, hp)) for v in (b1, b2, freq))
    w2p = _pad_to(w2, (hp, hp))
    w3p = _pad_to(w3, (hp, w3.shape[1]))
    deltas = jnp.abs(jnp.linspace(DECAY_MIN, DECAY_MAX, width, dtype=F32))[None, :]
    tc = _tile(width, 256)
    nc = width // tc
    small = lambda shape: pl.BlockSpec(shape, lambda c: (0,) * len(shape))
    w3spec = lambda g: pl.BlockSpec((hp, tc), lambda c, g=g: (0, g * nc + c))
    out3 = lambda rows: pl.BlockSpec((2, rows, tc), lambda c: (0, 0, c))
    return pl.pallas_call(
        functools.partial(_filter_kernel, seq=seq),
        grid=(nc,),
        in_specs=[small((1, LANES)), small((1, hp)), small((LANES, hp)), small((LANES, hp)), small((1, hp)),
                  small((hp, hp)), small((1, hp)), small((1, hp)), pl.BlockSpec((1, tc), lambda c: (0, c)),
                  w3spec(0), w3spec(1), w3spec(2), w3spec(3)],
        out_specs=[out3(seq), out3(seq), out3(1)],
        out_shape=[jax.ShapeDtypeStruct((2, seq, width), BF16), jax.ShapeDtypeStruct((2, seq, width), BF16),
                   jax.ShapeDtypeStruct((2, 1, width), F32)],
        scratch_shapes=[pltpu.VMEM((seq, hp), F32)],
        compiler_params=_params(("arbitrary",)),
        name="hyena_filters",
    )(fb, w1t, w1c, w1s, b1p, w2p, b2p, frp, deltas, w3p, w3p, w3p, w3p)


def _dft_tables(seq, fc):
    n = 2 * seq
    f = jnp.arange(seq, dtype=jnp.int32)[:, None]
    t = jnp.arange(seq, dtype=jnp.int32)[None, :]
    ang = ((f * t) & (n - 1)).astype(F32) * (2.0 * math.pi / n)
    cosm = jnp.cos(ang)
    sinm = jnp.where(f == 0, jnp.where(t % 2 == 0, 1.0, -1.0), -jnp.sin(ang))
    nk = seq // fc
    fwd = jnp.concatenate([cosm.reshape(nk, fc, seq), sinm.reshape(nk, fc, seq)], axis=1)
    inv = jnp.transpose(fwd, (0, 2, 1)) * (2.0 / n)
    return fwd.astype(BF16), inv.astype(BF16)


def _spectrum_kernel(w_ref, hs_ref, hd_ref, gr_ref, gi_ref, *, fc):
    gr_ref[0] = jnp.dot(w_ref[0, :fc, :], hs_ref[0], preferred_element_type=F32)
    gi_ref[0] = jnp.dot(w_ref[0, fc:, :], hd_ref[0], preferred_element_type=F32)


def _filter_spectrum(fwd, hs, hd):
    nk, fc2, seq = fwd.shape
    fc = fc2 // 2
    order, _, width = hs.shape
    tc = _tile(width, 512)
    hspec = pl.BlockSpec((1, seq, tc), lambda k, o, c: (o, 0, c))
    gspec = pl.BlockSpec((1, fc, tc), lambda k, o, c: (o, k, c))
    gshape = jax.ShapeDtypeStruct((order, seq, width), F32)
    return pl.pallas_call(
        functools.partial(_spectrum_kernel, fc=fc),
        grid=(nk, order, width // tc),
        in_specs=[pl.BlockSpec((1, fc2, seq), lambda k, o, c: (k, 0, 0)), hspec, hspec],
        out_specs=[gspec, gspec],
        out_shape=[gshape, gshape],
        compiler_params=_params(("arbitrary", "arbitrary", "arbitrary")),
        name="filter_spectrum",
    )(fwd, hs, hd)


def _hyena_kernel(v_ref, x1_ref, x2_ref, cwv_ref, cw1_ref, cw2_ref, cbv_ref, cb1_ref, cb2_ref,
                  fwd_ref, inv_ref, gr_ref, gi_ref, nq_ref, fbias_ref, o_ref,
                  u_ref, ub_ref, acc_ref, *, seq, fc):
    o, k = pl.program_id(2), pl.program_id(3)
    last = pl.num_programs(3) - 1
    row = lax.broadcasted_iota(jnp.int32, (seq, 1), 0)

    def short_conv(x_ref, w_ref, b_ref):
        x = x_ref[0].astype(F32)
        prev = jnp.where(row == 0, 0.0, pltpu.roll(x, 1, 0))
        nxt = jnp.where(row == seq - 1, 0.0, pltpu.roll(x, seq - 1, 0))
        return prev * w_ref[0:1, :] + x * w_ref[1:2, :] + nxt * w_ref[2:3, :] + b_ref[...]

    @pl.when((o == 0) & (k == 0))
    def _():
        v = short_conv(v_ref, cwv_ref, cbv_ref)
        u_ref[...] = v
        ub_ref[...] = v.astype(BF16)

    @pl.when(k == 0)
    def _():
        acc_ref[...] = jnp.zeros_like(acc_ref)

    spec = jnp.dot(fwd_ref[0], ub_ref[...], preferred_element_type=F32)
    vr, vi = spec[:fc], spec[fc:]
    gr, gi = gr_ref[0], gi_ref[0]
    special = (lax.broadcasted_iota(jnp.int32, (fc, 1), 0) == 0) & (k == 0)
    yr = jnp.where(special, 0.5 * vr * gr, vr * gr - vi * gi)
    yi = jnp.where(special, 0.5 * vi * nq_ref[0], vr * gi + vi * gr)
    y = jnp.concatenate([yr, yi], axis=0).astype(BF16)
    acc_ref[...] += jnp.dot(inv_ref[0], y, preferred_element_type=F32)

    @pl.when((k == last) & (o == 0))
    def _():
        z = (acc_ref[...] + u_ref[...] * fbias_ref[0]) * short_conv(x1_ref, cw1_ref, cb1_ref)
        u_ref[...] = z
        ub_ref[...] = z.astype(BF16)

    @pl.when((k == last) & (o == 1))
    def _():
        gate = short_conv(x2_ref, cw2_ref, cb2_ref)
        o_ref[0] = ((acc_ref[...] + u_ref[...] * fbias_ref[0]) * gate).astype(o_ref.dtype)


def _hyena_operator(proj, conv_w, conv_b, fwd, inv, gr, gi, nq, filt_bias, col0, width):
    B, L, _ = proj.shape
    nk, fc2, _ = fwd.shape
    tc = _tile(width, 512)
    nc = width // tc
    assert col0 % tc == 0
    xspec = lambda part: pl.BlockSpec((1, L, tc), lambda b, c, o, k, part=part: (b, 0, col0 // tc + part * nc + c),
                                      pipeline_mode=pl.Buffered(1))
    wspec = lambda rows, part: pl.BlockSpec((rows, tc), lambda b, c, o, k, part=part: (0, part * nc + c))
    ospec = lambda rows: pl.BlockSpec((1, rows, tc), lambda b, c, o, k: (o, 0, c))
    cb = conv_b.reshape(1, -1)
    return pl.pallas_call(
        functools.partial(_hyena_kernel, seq=L, fc=fc2 // 2),
        grid=(B, nc, 2, nk),
        in_specs=[xspec(0), xspec(1), xspec(2), wspec(SHORT_CONV, 0), wspec(SHORT_CONV, 1), wspec(SHORT_CONV, 2),
                  wspec(1, 0), wspec(1, 1), wspec(1, 2),
                  pl.BlockSpec((1, fc2, L), lambda b, c, o, k: (k, 0, 0)),
                  pl.BlockSpec((1, L, fc2), lambda b, c, o, k: (k, 0, 0)),
                  pl.BlockSpec((1, fc2 // 2, tc), lambda b, c, o, k: (o, k, c)),
                  pl.BlockSpec((1, fc2 // 2, tc), lambda b, c, o, k: (o, k, c)),
                  ospec(1), ospec(1)],
        out_specs=pl.BlockSpec((1, L, tc), lambda b, c, o, k: (b, 0, c)),
        out_shape=jax.ShapeDtypeStruct((B, L, width), BF16),
        scratch_shapes=[pltpu.VMEM((L, tc), F32), pltpu.VMEM((L, tc), BF16), pltpu.VMEM((L, tc), F32)],
        compiler_params=_params(("arbitrary",) * 4),
        name="hyena_operator",
    )(proj, proj, proj, conv_w, conv_w, conv_w, cb, cb, cb, fwd, inv, gr, gi, nq,
      filt_bias.reshape(2, 1, width))


def _merge_kernel(ga_ref, z_ref, gate_ref, wh_ref, wo_ref, h_ref, g_ref, h1_ref, hn_ref, hnt_ref):
    hy = jnp.dot(z_ref[...], wh_ref[0], preferred_element_type=F32)
    mix = ga_ref[...].astype(F32) + gate_ref[...].astype(F32) * hy
    h1 = h_ref[...] + _bdot(mix, wo_ref[0])
    h1_ref[...] = h1
    hn = _rms(h1, g_ref[...])
    hn_ref[...] = hn.astype(BF16)
    hnt_ref[...] = hn.T.astype(hnt_ref.dtype)


def _merge(ga, z2, proj, gate_col, whb, wob, layer, h, g):
    T, D = h.shape
    C = z2.shape[1]
    tm = _tile(T, 256)
    row = lambda w: pl.BlockSpec((tm, w), lambda i: (i, 0))
    full = lambda a: pl.BlockSpec(a.shape, lambda i: (0,) * a.ndim)
    g2 = g.reshape(1, D)
    return pl.pallas_call(
        _merge_kernel,
        grid=(T // tm,),
        in_specs=[row(D), row(C), pl.BlockSpec((tm, D), lambda i: (i, gate_col // D)),
                  _layer_spec(whb, layer), _layer_spec(wob, layer), row(D), full(g2)],
        out_specs=[row(D), row(D), pl.BlockSpec((D, tm), lambda i: (0, i))],
        out_shape=[jax.ShapeDtypeStruct((T, D), F32), jax.ShapeDtypeStruct((T, D), BF16),
                   jax.ShapeDtypeStruct((D, T), BF16)],
        compiler_params=_params(("parallel",)),
        name="mixer_merge",
    )(ga, z2, proj, whb, wob, h, g2)


SUBLANES = 8
PACKED_DTYPE = jnp.bfloat16
PACKED_ROWS = 2 * SUBLANES
WORD = jnp.uint32


def _sort_network(n):
    p = 1
    while p < n:
        p *= 2
    comps = []

    def merge(lo, cnt, r):
        step = 2 * r
        if step < cnt:
            merge(lo, cnt, step)
            merge(lo + r, cnt, step)
            comps.extend((i, i + r) for i in range(lo + r, lo + cnt - r, step))
        else:
            comps.append((lo, lo + r))

    def sort(lo, cnt):
        if cnt > 1:
            sort(lo, cnt // 2)
            sort(lo + cnt // 2, cnt // 2)
            merge(lo, cnt, 1)

    sort(0, p)
    return [(i, j) for i, j in comps if j < n]


def _top_sorted(x, count):
    lists = [x[SUBLANES * j:SUBLANES * (j + 1)] for j in range(x.shape[0] // SUBLANES)]
    for i, j in _sort_network(len(lists)):
        lists[i], lists[j] = jnp.maximum(lists[i], lists[j]), jnp.minimum(lists[i], lists[j])
    out = []
    for k in range(count):
        m = jnp.max(lists[0], axis=0, keepdims=True)
        out.append(m)
        hit = lists[0] == m
        for j in range(min(count - k - 1, len(lists))):
            below = lists[j + 1] if j + 1 < len(lists) else -jnp.inf
            lists[j] = jnp.where(hit, below, lists[j])
    return out


def _count_sorted(v, x, op):
    n = len(v)
    assert n & (n - 1) == 0
    count, taken, step = 0.0, [], n // 2
    while step >= 1:
        pivots = [v[base + step - 1] for base in range(0, n, 2 * step)]
        for t in taken:
            pivots = [jnp.where(t, hi, lo) for lo, hi in zip(pivots[0::2], pivots[1::2])]
        test = op(pivots[0], x)
        taken.insert(0, test)
        count = count + jnp.where(test, float(step), 0.0)
        step //= 2
    return count + jnp.where(op(v[n - 1], x), 1.0, 0.0)


def _route_kernel(hn_ref, wq_ref, k1_ref, k2_ref, r2_ref, e2_ref, cnt_ref, e1_ref, q_ref, *, heads, half):
    q_ref[...] = jnp.dot(hn_ref[...], wq_ref[0], preferred_element_type=F32)
    k1, k2 = k1_ref[0].astype(BF16), k2_ref[0].astype(BF16)
    nt = (((1,), (1,)), ((), ()))
    nbest = PEER_TOPK + 1
    padded = -(-nbest // SUBLANES) * SUBLANES
    assert nbest // 2 <= SUBLANES <= padded - SUBLANES
    tm = hn_ref.shape[0]

    def route_block(h, s1, s2, lanes):
        v1, v2 = _top_sorted(s1, nbest), _top_sorted(s2, nbest)
        pad = [jnp.full_like(v2[0], -jnp.inf)] * (padded - nbest)
        v1p, v2p = jnp.concatenate(v1 + pad, axis=0), jnp.concatenate(v2 + pad, axis=0)
        cand = jnp.concatenate([v1[0] + v2p, v1p[SUBLANES:] + v2[0]]
                               + [v1[a] + v2p[:SUBLANES] for a in range(1, SUBLANES)], axis=0)
        best = _top_sorted(cand, nbest)
        z = sum(jnp.exp(b - best[0]) for b in best[1:PEER_TOPK]) + 1.0
        thr = 0.5 * (best[PEER_TOPK - 1] + best[PEER_TOPK])
        need = thr - s1
        as_words = lambda x: pltpu.bitcast(x.astype(PACKED_DTYPE), WORD)
        r2_ref[h, :, lanes] = as_words(_count_sorted(v2[:PEER_TOPK], s2, jnp.greater))
        cnt_ref[h, :, lanes] = _count_sorted(v2[:PEER_TOPK], need, jnp.greater_equal)
        e2_ref[h, :, lanes] = as_words(jnp.exp(s2 - v2[0]) / z)
        e1_ref[h, :, lanes] = 0.5 * jnp.exp(s1 - v1[0])

    def route_head(h):
        base = pl.multiple_of(h * 2 * half, 2 * half)
        q1 = q_ref[:, pl.ds(base, half)].astype(BF16)
        q2 = q_ref[:, pl.ds(base + half, half)].astype(BF16)
        s1 = lax.dot_general(k1, q1, nt, preferred_element_type=F32)
        s2 = lax.dot_general(k2, q2, nt, preferred_element_type=F32)
        for tb in range(tm // LANES):
            lanes = slice(tb * LANES, (tb + 1) * LANES)
            route_block(h, s1[:, lanes], s2[:, lanes], lanes)

    group = 2 if heads % 2 == 0 else 1

    def body(g, carry):
        for r in range(group):
            route_head(g * group + r)
        return carry

    lax.fori_loop(0, heads // group, body, 0)


def _peer_route(hn, wqb, k1, k2, layer):
    T, D = hn.shape
    _, nk, half = k1.shape
    heads = wqb.shape[2] // (2 * half)
    tm = _tile(T, 512)
    ospec = lambda rows: pl.BlockSpec((heads, rows, tm), lambda i: (0, 0, i))
    wspec = lambda a: pl.BlockSpec((1,) + a.shape[1:], lambda i: (layer, 0, 0), pipeline_mode=pl.Buffered(1))
    return pl.pallas_call(
        functools.partial(_route_kernel, heads=heads, half=half),
        grid=(T // tm,),
        in_specs=[pl.BlockSpec((tm, D), lambda i: (i, 0)), wspec(wqb), wspec(k1), wspec(k2)],
        out_specs=[ospec(nk // 2), ospec(nk // 2), ospec(nk), ospec(nk)],
        out_shape=[jax.ShapeDtypeStruct((heads, nk // 2, T), WORD), jax.ShapeDtypeStruct((heads, nk // 2, T), WORD),
                   jax.ShapeDtypeStruct((heads, nk, T), F32), jax.ShapeDtypeStruct((heads, nk, T), F32)],
        scratch_shapes=[pltpu.VMEM((tm, wqb.shape[2]), F32)],
        compiler_params=_params(("parallel",)),
        name="peer_route",
    )(hn, wqb, k1, k2)


def _experts_kernel(hnt_ref, u_ref, v_ref, r2_ref, e2_ref, cnt_ref, e1_ref, o_ref, w_ref,
                    *, heads, nk, tm, te):
    j = pl.program_id(1)

    @pl.when(j == 0)
    def _():
        o_ref[...] = jnp.zeros_like(o_ref)

    groups = te // nk
    g0 = pl.multiple_of(j * groups, groups)
    na = 4

    nv = nk // PACKED_ROWS
    for tb in range(tm // LANES):
        lanes = slice(tb * LANES, (tb + 1) * LANES)
        for ag in range(groups // na):
            w = [[jnp.zeros((PACKED_ROWS, LANES), PACKED_DTYPE) for _ in range(nv)] for _ in range(na)]
            for h in range(heads):
                packed = lambda ref, c: pltpu.bitcast(ref[h, c * SUBLANES:(c + 1) * SUBLANES, lanes], PACKED_DTYPE)
                r2 = [packed(r2_ref, c) for c in range(nv)]
                e2 = [packed(e2_ref, c) for c in range(nv)]
                cg = cnt_ref[h, pl.ds(g0, groups), lanes]
                e1g = e1_ref[h, pl.ds(g0, groups), lanes]
                for a in range(na):
                    i = ag * na + a
                    cnt = jnp.broadcast_to(cg[i:i + 1], (PACKED_ROWS, LANES)).astype(PACKED_DTYPE)
                    e1 = jnp.broadcast_to(e1g[i:i + 1], (PACKED_ROWS, LANES)).astype(PACKED_DTYPE)
                    for c in range(nv):
                        w[a][c] = jnp.where(r2[c] < cnt, w[a][c] + e2[c] * e1, w[a][c])
            for a in range(na):
                for c in range(nv):
                    r0 = ((ag * na + a) * nk) // 2 + c * SUBLANES
                    w_ref[r0:r0 + SUBLANES, lanes] = pltpu.bitcast(w[a][c], WORD)

    nn = (((1,), (0,)), ((), ()))
    hid = lax.dot_general(u_ref[0], hnt_ref[...], nn, preferred_element_type=F32)
    w = pltpu.bitcast(w_ref[...], PACKED_DTYPE).astype(F32)
    at = hid * (1.0 + lax.erf(hid * (1.0 / math.sqrt(2.0)))) * w
    o_ref[...] += lax.dot_general(at.T.astype(BF16), v_ref[0], nn, preferred_element_type=F32)


def _peer_experts(hnt, u_tab, v_tab, r2, e2, cnt, e1, layer):
    D, T = hnt.shape
    E = u_tab.shape[1]
    heads, nk, _ = cnt.shape
    tm, te = _tile(T, 512), _tile(E, 8 * nk)
    assert te == 8 * nk
    once = dict(pipeline_mode=pl.Buffered(1))
    rspec = lambda a: pl.BlockSpec(a.shape[:2] + (tm,), lambda i, j: (0, 0, i), **once)
    tspec = pl.BlockSpec((1, te, D), lambda i, j: (layer, j, 0))
    return pl.pallas_call(
        functools.partial(_experts_kernel, heads=heads, nk=nk, tm=tm, te=te),
        grid=(T // tm, E // te),
        in_specs=[pl.BlockSpec((D, tm), lambda i, j: (0, i), **once), tspec, tspec,
                  rspec(r2), rspec(e2), rspec(cnt), rspec(e1)],
        out_specs=pl.BlockSpec((tm, D), lambda i, j: (i, 0), **once),
        out_shape=jax.ShapeDtypeStruct((T, D), F32),
        scratch_shapes=[pltpu.VMEM((te // 2, tm), WORD)],
        compiler_params=_params(("parallel", "arbitrary")),
        name="peer_experts",
    )(hnt, u_tab, v_tab, r2, e2, cnt, e1)


def _ple_kernel(h_ref, d_ref, g_ref, wg_ref, p_ref, wp_ref, gf_ref, o_ref, *, final):
    h = h_ref[...] + d_ref[...]
    xn = _rms(h, g_ref[...]).astype(BF16)
    gate = jax.nn.sigmoid(jnp.dot(xn, wg_ref[0], preferred_element_type=F32))
    out = h + gate * _bdot(p_ref[0], wp_ref[0])
    o_ref[...] = _rms(out, gf_ref[...]) if final else out


def _ple(h, delta, g, w_gate16, p, w_proj16, layer, final_g=None):
    T, D = h.shape
    pd = p.shape[2]
    tm = _tile(T, 512)
    row = pl.BlockSpec((tm, D), lambda i: (i, 0))
    vec = pl.BlockSpec((1, D), lambda i: (0, 0))
    gf = (g if final_g is None else final_g).reshape(1, D)
    return pl.pallas_call(
        functools.partial(_ple_kernel, final=final_g is not None),
        grid=(T // tm,),
        in_specs=[row, row, vec, _layer_spec(w_gate16, layer),
                  pl.BlockSpec((1, tm, pd), lambda i: (layer, i, 0)), _layer_spec(w_proj16, layer), vec],
        out_specs=row,
        out_shape=jax.ShapeDtypeStruct((T, D), F32),
        compiler_params=_params(("parallel",)),
        name="ple",
    )(h, delta, g.reshape(1, D), w_gate16, p, w_proj16, gf)


def kernel(x, p, norm_mix_g, w_in, pool_w, pool_scale, w_pool_out, short_conv_w, short_conv_b, filt_w1, filt_b1, filt_w2, filt_b2, filt_w3, filt_freq, filt_bias, w_hyena_out, w_o, norm_ffn_g, peer_wq, peer_k1, peer_k2, peer_u, peer_v, norm_ple_g, ple_w_gate, ple_w_proj, final_norm_g):
    B, L, D = x.shape
    T = B * L
    depth = w_in.shape[0]
    pw = pool_scale.shape[-1]
    C = w_hyena_out.shape[1]
    hy_col, gate_col = pw, pw + 3 * C
    fwd, inv = _dft_tables(L, _tile(L, 512))
    pool_w16, wpo16, why16, wo16, wq16, wg16, wp16 = (
        w.astype(BF16) for w in (pool_w, w_pool_out, w_hyena_out, w_o, peer_wq, ple_w_gate, ple_w_proj))
    p3 = p.reshape(depth, T, -1)
    h = x.reshape(T, D)
    for i in range(depth):
        proj = _inproj(h, norm_mix_g[i], w_in, i, 2 * D).reshape(B, L, -1)
        ga = _pool_branch(proj, pool_w16, pool_scale[i], wpo16, i, gate_col)
        hs, hd, nq = _hyena_filters(L, filt_w1[i], filt_b1[i], filt_w2[i], filt_b2[i], filt_w3[i], filt_freq[i], C)
        gr, gi = _filter_spectrum(fwd, hs, hd)
        z2 = _hyena_operator(proj, short_conv_w[i], short_conv_b[i], fwd, inv, gr, gi, nq, filt_bias[i], hy_col, C)
        h1, hn, hnt = _merge(ga.reshape(T, D), z2.reshape(T, C), proj.reshape(T, -1), gate_col + D,
                             why16, wo16, i, h, norm_ffn_g[i])
        r2, e2, cnt, e1 = _peer_route(hn, wq16, peer_k1, peer_k2, i)
        po = _peer_experts(hnt, peer_u, peer_v, r2, e2, cnt, e1, i)
        h = _ple(h1, po, norm_ple_g[i], wg16, p3, wp16, i, final_norm_g if i == depth - 1 else None)
    return h.reshape(B, L, D)
```

```python
import functools
import math

import jax
import jax.numpy as jnp
from jax import lax
from jax.experimental import pallas as pl
from jax.experimental.pallas import tpu as pltpu

F32 = jnp.float32
BF16 = jnp.bfloat16

NORM_EPS = 1e-6
POOL_WINDOWS = (2, 4, 8, 16)
SHORT_CONV = 3
FILTER_EMB = 33
DECAY_TARGET = 1e-2
DECAY_MIN = math.log(DECAY_TARGET) / 1.5
DECAY_MAX = math.log(DECAY_TARGET) / 0.3
PEER_TOPK = 16
LANES = 128
SUBLANES = 8
V7X_VMEM_BYTES = 64 * 1024 * 1024
VMEM_LIMIT = V7X_VMEM_BYTES - 8 * 1024 * 1024
HI = lax.Precision.HIGHEST


def _tile(n, pref):
    return pref if n % pref == 0 else n


def _params(sem, **options):
    return pltpu.CompilerParams(dimension_semantics=sem, vmem_limit_bytes=VMEM_LIMIT, **options)


def _bdot(a, b):
    return jnp.dot(a.astype(BF16), b.astype(BF16), preferred_element_type=F32)


def _rms(x, g):
    ms = jnp.mean(x * x, axis=-1, keepdims=True)
    return x * lax.rsqrt(ms + NORM_EPS) * g


def _inproj_kernel(x_ref, g_ref, w_ref, o_ref, xn_ref, *, n_plain):
    j = pl.program_id(1)

    @pl.when(j == 0)
    def _():
        xn_ref[...] = _rms(x_ref[...], g_ref[...]).astype(BF16)

    acc = jnp.dot(xn_ref[...], w_ref[0].astype(BF16), preferred_element_type=F32)
    o_ref[...] = jnp.where(j >= n_plain, jax.nn.sigmoid(acc), acc).astype(o_ref.dtype)


def _inproj(h, g, w, layer, n_gate_cols):
    T, D = h.shape
    N = w.shape[2]
    tm = _tile(T, 1024)
    tn = next(t for t in (1024, 512, 256, LANES) if N % t == 0 and (N - n_gate_cols) % t == 0)
    return pl.pallas_call(
        functools.partial(_inproj_kernel, n_plain=(N - n_gate_cols) // tn),
        grid=(T // tm, N // tn),
        in_specs=[pl.BlockSpec((tm, D), lambda i, j: (i, 0)),
                  pl.BlockSpec((1, D), lambda i, j: (0, 0)),
                  pl.BlockSpec((1, D, tn), lambda i, j: (layer, 0, j))],
        out_specs=pl.BlockSpec((tm, tn), lambda i, j: (i, j)),
        out_shape=jax.ShapeDtypeStruct((T, N), BF16),
        scratch_shapes=[pltpu.VMEM((tm, D), BF16)],
        compiler_params=_params(("parallel", "arbitrary")),
        name="inproj",
    )(h, g.reshape(1, D), w)


def _pool_kernel(pin_ref, gate_ref, pw_ref, ps_ref, wo_ref, o_ref, *, tm, kw, seq):
    t0 = pl.program_id(1) * tm
    k0 = pl.multiple_of(jnp.clip(t0 - (kw - tm) // 2, 0, seq - kw), LANES)
    r = t0 + lax.broadcasted_iota(jnp.int32, (tm, 1), 0)
    s = k0 + lax.broadcasted_iota(jnp.int32, (1, kw), 1)
    group = pw_ref.shape[2]
    parts = []
    for gi, win in enumerate(POOL_WINDOWS):
        lo = jnp.clip(r - win // 2, 0, seq - 1)
        hi = jnp.clip(r + win - win // 2 - 1, 0, seq - 1)
        inv = 1.0 / (hi - lo + 1).astype(F32)
        band = jnp.where((s >= lo) & (s <= hi), inv, 0.0) - jnp.where(s == r, 1.0, 0.0)
        u = pin_ref[0, pl.ds(k0, kw), gi * group:(gi + 1) * group]
        centred = jnp.dot(band.astype(BF16), u, preferred_element_type=F32)
        parts.append(_bdot(centred, pw_ref[0, gi]))
    pooled = jnp.concatenate(parts, axis=-1) * ps_ref[...]
    a = _bdot(pooled, wo_ref[0])
    o_ref[0] = (a * gate_ref[0].astype(F32)).astype(o_ref.dtype)


def _layer_spec(a, layer):
    nd = a.ndim - 1
    return pl.BlockSpec((1,) + a.shape[1:], lambda *_: (layer,) + (0,) * nd, pipeline_mode=pl.Buffered(1))


def _pool_branch(proj, pool_w16, pool_scale, w_pool_out16, layer, gate_col):
    B, L, _ = proj.shape
    _, pw, D = w_pool_out16.shape
    tm = _tile(L, 256)
    kw = min(L, 2 * tm)
    assert max(POOL_WINDOWS) <= (kw - tm) // 2 or kw == L
    return pl.pallas_call(
        functools.partial(_pool_kernel, tm=tm, kw=kw, seq=L),
        grid=(B, L // tm),
        in_specs=[pl.BlockSpec((1, L, pw), lambda b, i: (b, 0, 0)),
                  pl.BlockSpec((1, tm, D), lambda b, i: (b, i, gate_col // D)),
                  _layer_spec(pool_w16, layer),
                  pl.BlockSpec((1, pw), lambda b, i: (0, 0)),
                  _layer_spec(w_pool_out16, layer)],
        out_specs=pl.BlockSpec((1, tm, D), lambda b, i: (b, i, 0)),
        out_shape=jax.ShapeDtypeStruct((B, L, D), BF16),
        compiler_params=_params(("parallel", "arbitrary")),
        name="pool_branch",
    )(proj, proj, pool_w16, pool_scale.reshape(1, pw), w_pool_out16)


def _filter_kernel(fb_ref, w1t_ref, w1c_ref, w1s_ref, b1_ref, w2_ref, b2_ref, fr_ref, dl_ref,
                   w3a_ref, w3b_ref, w3c_ref, w3d_ref, hs_ref, hd_ref, nq_ref, hdn_ref, *, seq):
    @pl.when(pl.program_id(0) == 0)
    def _():
        ti = lax.broadcasted_iota(jnp.int32, (seq, 1), 0).astype(F32)
        t = ti / (seq - 1.0)
        ang = (2.0 * math.pi / seq) * ti * fb_ref[...]
        z = (t * w1t_ref[...] + jnp.dot(jnp.cos(ang), w1c_ref[...], precision=HI, preferred_element_type=F32)
             - jnp.dot(jnp.sin(ang), w1s_ref[...], precision=HI, preferred_element_type=F32))
        hdn = jnp.sin(fr_ref[...] * (z + b1_ref[...]))
        hdn = jnp.sin(fr_ref[...] * (jnp.dot(hdn, w2_ref[...], precision=HI, preferred_element_type=F32)
                                    + b2_ref[...]))
        hdn_ref[...] = hdn

    ti = lax.broadcasted_iota(jnp.int32, (seq, 1), 0)
    t = ti.astype(F32) / (seq - 1.0)
    decay = jnp.exp(-t * dl_ref[...])
    sign = jnp.where(ti % 2 == 0, 1.0, -1.0)

    def filt(w3_ref):
        f = jnp.dot(hdn_ref[...], w3_ref[...], precision=HI, preferred_element_type=F32) * decay
        return f / (jnp.sum(jnp.abs(f), axis=0, keepdims=True) + 1e-6)

    for order, (wf_ref, wb_ref) in enumerate(((w3a_ref, w3c_ref), (w3b_ref, w3d_ref))):
        hf, hb = filt(wf_ref), filt(wb_ref)
        hsum = hf + hb
        hs_ref[order] = hsum.astype(hs_ref.dtype)
        hd_ref[order] = (hf - hb).astype(hd_ref.dtype)
        nq_ref[order] = jnp.sum(hsum * sign, axis=0, keepdims=True)


def _pad_to(a, shape):
    return jnp.pad(a, [(0, s - d) for d, s in zip(a.shape, shape)])


def _hyena_filters(seq, w1, b1, w2, b2, w3, freq, width):
    hid = w1.shape[1]
    bands = (FILTER_EMB - 1) // 2
    hp = max(LANES, hid)
    fb = _pad_to(jnp.linspace(1e-4, bands - 1, bands, dtype=F32)[None, :], (1, LANES))
    w1t = _pad_to(w1[0:1], (1, hp))
    w1c = _pad_to(w1[1:1 + bands], (LANES, hp))
    w1s = _pad_to(w1[1 + bands:], (LANES, hp))
    b1p, b2p, frp = (_pad_to(v.reshape(1, hid), (1, hp)) for v in (b1, b2, freq))
    w2p = _pad_to(w2, (hp, hp))
    w3p = _pad_to(w3, (hp, w3.shape[1]))
    deltas = jnp.abs(jnp.linspace(DECAY_MIN, DECAY_MAX, width, dtype=F32))[None, :]
    tc = _tile(width, 256)
    nc = width // tc
    small = lambda shape: pl.BlockSpec(shape, lambda c: (0,) * len(shape))
    w3spec = lambda g: pl.BlockSpec((hp, tc), lambda c, g=g: (0, g * nc + c))
    out3 = lambda rows: pl.BlockSpec((2, rows, tc), lambda c: (0, 0, c))
    return pl.pallas_call(
        functools.partial(_filter_kernel, seq=seq),
        grid=(nc,),
        in_specs=[small((1, LANES)), small((1, hp)), small((LANES, hp)), small((LANES, hp)), small((1, hp)),
                  small((hp, hp)), small((1, hp)), small((1, hp)), pl.BlockSpec((1, tc), lambda c: (0, c)),
                  w3spec(0), w3spec(1), w3spec(2), w3spec(3)],
        out_specs=[out3(seq), out3(seq), out3(1)],
        out_shape=[jax.ShapeDtypeStruct((2, seq, width), BF16), jax.ShapeDtypeStruct((2, seq, width), BF16),
                   jax.ShapeDtypeStruct((2, 1, width), F32)],
        scratch_shapes=[pltpu.VMEM((seq, hp), F32)],
        compiler_params=_params(("arbitrary",)),
        name="hyena_filters",
    )(fb, w1t, w1c, w1s, b1p, w2p, b2p, frp, deltas, w3p, w3p, w3p, w3p)


def _dft_tables(seq, fc):
    n = 2 * seq
    t = jnp.arange(seq, dtype=jnp.int32)[None, :]
    assert n & (n - 1) == 0
    fine = 1 << (seq.bit_length() // 2)

    def trig(freqs):
        ang = ((freqs[:, None] * t) & (n - 1)).astype(F32) * (2.0 * math.pi / n)
        return jnp.cos(ang), jnp.sin(ang)

    (ch, sh), (cl, sl) = trig(fine * jnp.arange(seq // fine, dtype=jnp.int32)), trig(jnp.arange(fine, dtype=jnp.int32))
    cosm = (ch[:, None] * cl[None] - sh[:, None] * sl[None]).reshape(seq, seq)
    sinm = -(sh[:, None] * cl[None] + ch[:, None] * sl[None]).reshape(seq, seq)
    f = jnp.arange(seq, dtype=jnp.int32)[:, None]
    sinm = jnp.where(f == 0, jnp.where(t % 2 == 0, 1.0, -1.0), sinm)
    nk = seq // fc
    fwd = jnp.concatenate([cosm.reshape(nk, fc, seq), sinm.reshape(nk, fc, seq)], axis=1)
    inv = jnp.transpose(fwd, (0, 2, 1)) * (2.0 / n)
    return fwd.astype(BF16), inv.astype(BF16)


def _spectrum_kernel(w_ref, hs_ref, hd_ref, gr_ref, gi_ref, *, fc):
    gr_ref[0] = jnp.dot(w_ref[0, :fc, :], hs_ref[0], preferred_element_type=F32)
    gi_ref[0] = jnp.dot(w_ref[0, fc:, :], hd_ref[0], preferred_element_type=F32)


def _filter_spectrum(fwd, hs, hd):
    nk, fc2, seq = fwd.shape
    fc = fc2 // 2
    order, _, width = hs.shape
    tc = _tile(width, 512)
    hspec = pl.BlockSpec((1, seq, tc), lambda k, o, c: (o, 0, c))
    gspec = pl.BlockSpec((1, fc, tc), lambda k, o, c: (o, k, c))
    gshape = jax.ShapeDtypeStruct((order, seq, width), F32)
    return pl.pallas_call(
        functools.partial(_spectrum_kernel, fc=fc),
        grid=(nk, order, width // tc),
        in_specs=[pl.BlockSpec((1, fc2, seq), lambda k, o, c: (k, 0, 0)), hspec, hspec],
        out_specs=[gspec, gspec],
        out_shape=[gshape, gshape],
        compiler_params=_params(("arbitrary", "arbitrary", "arbitrary")),
        name="filter_spectrum",
    )(fwd, hs, hd)


def _hyena_kernel(v_ref, x1_ref, x2_ref, cwv_ref, cw1_ref, cw2_ref, cbv_ref, cb1_ref, cb2_ref,
                  fwd_ref, inv_ref, gr_ref, gi_ref, nq_ref, fbias_ref, o_ref,
                  u_ref, ub_ref, acc_ref, *, seq, fc):
    o, k = pl.program_id(2), pl.program_id(3)
    last = pl.num_programs(3) - 1
    row = lax.broadcasted_iota(jnp.int32, (seq, 1), 0)

    def short_conv(x_ref, w_ref, b_ref):
        x = x_ref[0].astype(F32)
        prev = jnp.where(row == 0, 0.0, pltpu.roll(x, 1, 0))
        nxt = jnp.where(row == seq - 1, 0.0, pltpu.roll(x, seq - 1, 0))
        return prev * w_ref[0:1, :] + x * w_ref[1:2, :] + nxt * w_ref[2:3, :] + b_ref[...]

    @pl.when((o == 0) & (k == 0))
    def _():
        v = short_conv(v_ref, cwv_ref, cbv_ref)
        u_ref[...] = v
        ub_ref[...] = v.astype(BF16)

    @pl.when(k == 0)
    def _():
        acc_ref[...] = jnp.zeros_like(acc_ref)

    spec = jnp.dot(fwd_ref[0], ub_ref[...], preferred_element_type=F32)
    vr, vi = spec[:fc], spec[fc:]
    gr, gi = gr_ref[0], gi_ref[0]
    yr, yi = vr * gr - vi * gi, vr * gi + vi * gr
    top = slice(0, SUBLANES)
    special = (lax.broadcasted_iota(jnp.int32, (SUBLANES, 1), 0) == 0) & (k == 0)
    yr0 = jnp.where(special, 0.5 * vr[top] * gr[top], yr[top])
    yi0 = jnp.where(special, 0.5 * vi[top] * nq_ref[0], yi[top])
    y = jnp.concatenate([yr0, yr[SUBLANES:], yi0, yi[SUBLANES:]], axis=0).astype(BF16)
    acc_ref[...] += jnp.dot(inv_ref[0], y, preferred_element_type=F32)

    @pl.when((k == last) & (o == 0))
    def _():
        z = (acc_ref[...] + u_ref[...] * fbias_ref[0]) * short_conv(x1_ref, cw1_ref, cb1_ref)
        u_ref[...] = z
        ub_ref[...] = z.astype(BF16)

    @pl.when((k == last) & (o == 1))
    def _():
        gate = short_conv(x2_ref, cw2_ref, cb2_ref)
        o_ref[0] = ((acc_ref[...] + u_ref[...] * fbias_ref[0]) * gate).astype(o_ref.dtype)


def _hyena_operator(proj, conv_w, conv_b, fwd, inv, gr, gi, nq, filt_bias, col0, width):
    B, L, _ = proj.shape
    nk, fc2, _ = fwd.shape
    tc = _tile(width, 512)
    nc = width // tc
    assert col0 % tc == 0
    xspec = lambda part: pl.BlockSpec((1, L, tc), lambda b, c, o, k, part=part: (b, 0, col0 // tc + part * nc + c),
                                      pipeline_mode=pl.Buffered(1))
    wspec = lambda rows, part: pl.BlockSpec((rows, tc), lambda b, c, o, k, part=part: (0, part * nc + c))
    ospec = lambda rows: pl.BlockSpec((1, rows, tc), lambda b, c, o, k: (o, 0, c))
    cb = conv_b.reshape(1, -1)
    return pl.pallas_call(
        functools.partial(_hyena_kernel, seq=L, fc=fc2 // 2),
        grid=(B, nc, 2, nk),
        in_specs=[xspec(0), xspec(1), xspec(2), wspec(SHORT_CONV, 0), wspec(SHORT_CONV, 1), wspec(SHORT_CONV, 2),
                  wspec(1, 0), wspec(1, 1), wspec(1, 2),
                  pl.BlockSpec((1, fc2, L), lambda b, c, o, k: (k, 0, 0)),
                  pl.BlockSpec((1, L, fc2), lambda b, c, o, k: (k, 0, 0)),
                  pl.BlockSpec((1, fc2 // 2, tc), lambda b, c, o, k: (o, k, c)),
                  pl.BlockSpec((1, fc2 // 2, tc), lambda b, c, o, k: (o, k, c)),
                  ospec(1), ospec(1)],
        out_specs=pl.BlockSpec((1, L, tc), lambda b, c, o, k: (b, 0, c)),
        out_shape=jax.ShapeDtypeStruct((B, L, width), BF16),
        scratch_shapes=[pltpu.VMEM((L, tc), F32), pltpu.VMEM((L, tc), BF16), pltpu.VMEM((L, tc), F32)],
        compiler_params=_params(("arbitrary",) * 4),
        name="hyena_operator",
    )(proj, proj, proj, conv_w, conv_w, conv_w, cb, cb, cb, fwd, inv, gr, gi, nq,
      filt_bias.reshape(2, 1, width))


def _merge_kernel(ga_ref, z_ref, gate_ref, wh_ref, wo_ref, h_ref, g_ref, h1_ref, hn_ref, hnt_ref):
    hy = jnp.dot(z_ref[...], wh_ref[0], preferred_element_type=F32)
    mix = ga_ref[...].astype(F32) + gate_ref[...].astype(F32) * hy
    h1 = h_ref[...] + _bdot(mix, wo_ref[0])
    h1_ref[...] = h1
    hn = _rms(h1, g_ref[...])
    hn_ref[...] = hn.astype(BF16)
    hnt_ref[...] = hn.T.astype(hnt_ref.dtype)


def _merge(ga, z2, proj, gate_col, whb, wob, layer, h, g):
    T, D = h.shape
    C = z2.shape[1]
    tm = _tile(T, 512)
    row = lambda w: pl.BlockSpec((tm, w), lambda i: (i, 0))
    full = lambda a: pl.BlockSpec(a.shape, lambda i: (0,) * a.ndim)
    g2 = g.reshape(1, D)
    return pl.pallas_call(
        _merge_kernel,
        grid=(T // tm,),
        in_specs=[row(D), row(C), pl.BlockSpec((tm, D), lambda i: (i, gate_col // D)),
                  _layer_spec(whb, layer), _layer_spec(wob, layer), row(D), full(g2)],
        out_specs=[row(D), row(D), pl.BlockSpec((D, tm), lambda i: (0, i))],
        out_shape=[jax.ShapeDtypeStruct((T, D), F32), jax.ShapeDtypeStruct((T, D), BF16),
                   jax.ShapeDtypeStruct((D, T), BF16)],
        compiler_params=_params(("parallel",)),
        name="mixer_merge",
    )(ga, z2, proj, whb, wob, h, g2)


PACKED_DTYPE = jnp.bfloat16
PACKED_ROWS = 2 * SUBLANES
WORD = jnp.uint32


def _sort_network(n):
    p = 1
    while p < n:
        p *= 2
    comps = []

    def merge(lo, cnt, r):
        step = 2 * r
        if step < cnt:
            merge(lo, cnt, step)
            merge(lo + r, cnt, step)
            comps.extend((i, i + r) for i in range(lo + r, lo + cnt - r, step))
        else:
            comps.append((lo, lo + r))

    def sort(lo, cnt):
        if cnt > 1:
            sort(lo, cnt // 2)
            sort(lo + cnt // 2, cnt // 2)
            merge(lo, cnt, 1)

    sort(0, p)
    return [(i, j) for i, j in comps if j < n]


def _top_sorted(x, count):
    lists = [x[SUBLANES * j:SUBLANES * (j + 1)] for j in range(x.shape[0] // SUBLANES)]
    for i, j in _sort_network(len(lists)):
        lists[i], lists[j] = jnp.maximum(lists[i], lists[j]), jnp.minimum(lists[i], lists[j])
    out = []
    for k in range(count):
        m = jnp.max(lists[0], axis=0, keepdims=True)
        out.append(m)
        hit = lists[0] == m
        for j in range(min(count - k - 1, len(lists))):
            below = lists[j + 1] if j + 1 < len(lists) else -jnp.inf
            lists[j] = jnp.where(hit, below, lists[j])
    return out


def _count_sorted(v, x, op):
    n = len(v)
    assert n & (n - 1) == 0
    count, taken, step = 0.0, [], n // 2
    while step >= 1:
        pivots = [v[base + step - 1] for base in range(0, n, 2 * step)]
        for t in taken:
            pivots = [jnp.where(t, hi, lo) for lo, hi in zip(pivots[0::2], pivots[1::2])]
        test = op(pivots[0], x)
        taken.insert(0, test)
        count = count + jnp.where(test, float(step), 0.0)
        step //= 2
    return count + jnp.where(op(v[n - 1], x), 1.0, 0.0)


def _route_kernel(hn_ref, wq_ref, k1_ref, k2_ref, r2_ref, e2_ref, cnt_ref, e1_ref, q_ref, *, heads, half):
    q_ref[...] = jnp.dot(hn_ref[...], wq_ref[0], preferred_element_type=F32)
    k1, k2 = k1_ref[0].astype(BF16), k2_ref[0].astype(BF16)
    nt = (((1,), (1,)), ((), ()))
    nbest = PEER_TOPK + 1
    padded = -(-nbest // SUBLANES) * SUBLANES
    assert nbest // 2 <= SUBLANES <= padded - SUBLANES
    tm = hn_ref.shape[0]

    def route_block(h, s1, s2, lanes):
        v1, v2 = _top_sorted(s1, nbest), _top_sorted(s2, nbest)
        pad = [jnp.full_like(v2[0], -jnp.inf)] * (padded - nbest)
        v1p, v2p = jnp.concatenate(v1 + pad, axis=0), jnp.concatenate(v2 + pad, axis=0)
        cand = jnp.concatenate([v1[0] + v2p, v1p[SUBLANES:] + v2[0]]
                               + [v1[a] + v2p[:SUBLANES] for a in range(1, SUBLANES)], axis=0)
        best = _top_sorted(cand, nbest)
        z = sum(jnp.exp(b - best[0]) for b in best[1:PEER_TOPK]) + 1.0
        thr = 0.5 * (best[PEER_TOPK - 1] + best[PEER_TOPK])
        need = thr - s1
        as_words = lambda x: pltpu.bitcast(x.astype(PACKED_DTYPE), WORD)
        r2_ref[h, :, lanes] = as_words(_count_sorted(v2[:PEER_TOPK], s2, jnp.greater))
        cnt_ref[h, :, lanes] = _count_sorted(v2[:PEER_TOPK], need, jnp.greater_equal)
        e2_ref[h, :, lanes] = as_words(jnp.exp(s2 - v2[0]) / z)
        e1_ref[h, :, lanes] = 0.5 * jnp.exp(s1 - v1[0])

    def route_head(h):
        base = pl.multiple_of(h * 2 * half, 2 * half)
        q1 = q_ref[:, pl.ds(base, half)].astype(BF16)
        q2 = q_ref[:, pl.ds(base + half, half)].astype(BF16)
        s1 = lax.dot_general(k1, q1, nt, preferred_element_type=F32)
        s2 = lax.dot_general(k2, q2, nt, preferred_element_type=F32)
        for tb in range(tm // LANES):
            lanes = slice(tb * LANES, (tb + 1) * LANES)
            route_block(h, s1[:, lanes], s2[:, lanes], lanes)

    group = 4 if heads % 4 == 0 else 1

    def body(g, carry):
        for r in range(group):
            route_head(g * group + r)
        return carry

    lax.fori_loop(0, heads // group, body, 0)


def _peer_route(hn, wqb, k1, k2, layer):
    T, D = hn.shape
    _, nk, half = k1.shape
    heads = wqb.shape[2] // (2 * half)
    tm = _tile(T, 512)
    ospec = lambda rows: pl.BlockSpec((heads, rows, tm), lambda i: (0, 0, i))
    wspec = lambda a: pl.BlockSpec((1,) + a.shape[1:], lambda i: (layer, 0, 0), pipeline_mode=pl.Buffered(1))
    return pl.pallas_call(
        functools.partial(_route_kernel, heads=heads, half=half),
        grid=(T // tm,),
        in_specs=[pl.BlockSpec((tm, D), lambda i: (i, 0)), wspec(wqb), wspec(k1), wspec(k2)],
        out_specs=[ospec(nk // 2), ospec(nk // 2), ospec(nk), ospec(nk)],
        out_shape=[jax.ShapeDtypeStruct((heads, nk // 2, T), WORD), jax.ShapeDtypeStruct((heads, nk // 2, T), WORD),
                   jax.ShapeDtypeStruct((heads, nk, T), F32), jax.ShapeDtypeStruct((heads, nk, T), F32)],
        scratch_shapes=[pltpu.VMEM((tm, wqb.shape[2]), F32)],
        compiler_params=_params(("parallel",)),
        name="peer_route",
    )(hn, wqb, k1, k2)


def _experts_kernel(hnt_ref, u_ref, v_ref, r2_ref, e2_ref, cnt_ref, e1_ref, o_ref, w_ref,
                    *, heads, nk, tm, te):
    j = pl.program_id(1)

    @pl.when(j == 0)
    def _():
        o_ref[...] = jnp.zeros_like(o_ref)

    groups = te // nk
    g0 = pl.multiple_of(j * groups, groups)
    na = 4

    nv = nk // PACKED_ROWS
    for tb in range(tm // LANES):
        lanes = slice(tb * LANES, (tb + 1) * LANES)
        for ag in range(groups // na):
            w = [[0.0] * nv for _ in range(na)]
            for h in range(heads):
                packed = lambda ref, c: pltpu.bitcast(ref[h, c * SUBLANES:(c + 1) * SUBLANES, lanes], PACKED_DTYPE)
                r2 = [packed(r2_ref, c) for c in range(nv)]
                e2 = [packed(e2_ref, c) for c in range(nv)]
                cg = cnt_ref[h, pl.ds(g0, groups), lanes]
                e1g = e1_ref[h, pl.ds(g0, groups), lanes]
                for a in range(na):
                    i = ag * na + a
                    cnt = jnp.broadcast_to(cg[i:i + 1], (PACKED_ROWS, LANES)).astype(PACKED_DTYPE)
                    e1 = jnp.broadcast_to(e1g[i:i + 1], (PACKED_ROWS, LANES)).astype(PACKED_DTYPE)
                    for c in range(nv):
                        w[a][c] = jnp.where(r2[c] < cnt, w[a][c] + e2[c] * e1, w[a][c]).astype(PACKED_DTYPE)
            for a in range(na):
                for c in range(nv):
                    r0 = ((ag * na + a) * nk) // 2 + c * SUBLANES
                    w_ref[r0:r0 + SUBLANES, lanes] = pltpu.bitcast(w[a][c], WORD)

    nn = (((1,), (0,)), ((), ()))
    hid = lax.dot_general(u_ref[0], hnt_ref[...], nn, preferred_element_type=F32)
    w = pltpu.bitcast(w_ref[...], PACKED_DTYPE).astype(F32)
    at = hid * (1.0 + lax.erf(hid * (1.0 / math.sqrt(2.0)))) * w
    o_ref[...] += lax.dot_general(at.astype(BF16).T, v_ref[0], nn, preferred_element_type=F32)


def _peer_experts(hnt, u_tab, v_tab, r2, e2, cnt, e1, layer):
    D, T = hnt.shape
    E = u_tab.shape[1]
    heads, nk, _ = cnt.shape
    tm, te = _tile(T, 512), _tile(E, 8 * nk)
    assert te == 8 * nk
    once = dict(pipeline_mode=pl.Buffered(1))
    rspec = lambda a: pl.BlockSpec(a.shape[:2] + (tm,), lambda i, j: (0, 0, i), **once)
    tspec = pl.BlockSpec((1, te, D), lambda i, j: (layer, j, 0))
    return pl.pallas_call(
        functools.partial(_experts_kernel, heads=heads, nk=nk, tm=tm, te=te),
        grid=(T // tm, E // te),
        in_specs=[pl.BlockSpec((D, tm), lambda i, j: (0, i), **once), tspec, tspec,
                  rspec(r2), rspec(e2), rspec(cnt), rspec(e1)],
        out_specs=pl.BlockSpec((tm, D), lambda i, j: (i, 0), **once),
        out_shape=jax.ShapeDtypeStruct((T, D), F32),
        scratch_shapes=[pltpu.VMEM((te // 2, tm), WORD)],
        compiler_params=_params(("parallel", "arbitrary")),
        name="peer_experts",
    )(hnt, u_tab, v_tab, r2, e2, cnt, e1)


def _ple_kernel(h_ref, d_ref, g_ref, wg_ref, p_ref, wp_ref, gf_ref, o_ref, *, final):
    h = h_ref[...] + d_ref[...]
    xn = _rms(h, g_ref[...]).astype(BF16)
    gate = jax.nn.sigmoid(jnp.dot(xn, wg_ref[0], preferred_element_type=F32))
    out = h + gate * _bdot(p_ref[0], wp_ref[0])
    o_ref[...] = _rms(out, gf_ref[...]) if final else out


def _ple(h, delta, g, w_gate16, p, w_proj16, layer, final_g=None):
    T, D = h.shape
    pd = p.shape[2]
    tm = _tile(T, 512)
    row = pl.BlockSpec((tm, D), lambda i: (i, 0))
    vec = pl.BlockSpec((1, D), lambda i: (0, 0))
    gf = (g if final_g is None else final_g).reshape(1, D)
    return pl.pallas_call(
        functools.partial(_ple_kernel, final=final_g is not None),
        grid=(T // tm,),
        in_specs=[row, row, vec, _layer_spec(w_gate16, layer),
                  pl.BlockSpec((1, tm, pd), lambda i: (layer, i, 0)), _layer_spec(w_proj16, layer), vec],
        out_specs=row,
        out_shape=jax.ShapeDtypeStruct((T, D), F32),
        compiler_params=_params(("parallel",)),
        name="ple",
    )(h, delta, g.reshape(1, D), w_gate16, p, w_proj16, gf)


def kernel(x, p, norm_mix_g, w_in, pool_w, pool_scale, w_pool_out, short_conv_w, short_conv_b, filt_w1, filt_b1, filt_w2, filt_b2, filt_w3, filt_freq, filt_bias, w_hyena_out, w_o, norm_ffn_g, peer_wq, peer_k1, peer_k2, peer_u, peer_v, norm_ple_g, ple_w_gate, ple_w_proj, final_norm_g):
    B, L, D = x.shape
    T = B * L
    depth = w_in.shape[0]
    pw = pool_scale.shape[-1]
    C = w_hyena_out.shape[1]
    hy_col, gate_col = pw, pw + 3 * C
    fwd, inv = _dft_tables(L, _tile(L, 512))
    pool_w16, wpo16, why16, wo16, wq16, wg16, wp16 = (
        w.astype(BF16) for w in (pool_w, w_pool_out, w_hyena_out, w_o, peer_wq, ple_w_gate, ple_w_proj))
    p3 = p.reshape(depth, T, -1)
    h = x.reshape(T, D)
    for i in range(depth):
        proj = _inproj(h, norm_mix_g[i], w_in, i, 2 * D).reshape(B, L, -1)
        ga = _pool_branch(proj, pool_w16, pool_scale[i], wpo16, i, gate_col)
        hs, hd, nq = _hyena_filters(L, filt_w1[i], filt_b1[i], filt_w2[i], filt_b2[i], filt_w3[i], filt_freq[i], C)
        gr, gi = _filter_spectrum(fwd, hs, hd)
        z2 = _hyena_operator(proj, short_conv_w[i], short_conv_b[i], fwd, inv, gr, gi, nq, filt_bias[i], hy_col, C)
        h1, hn, hnt = _merge(ga.reshape(T, D), z2.reshape(T, C), proj.reshape(T, -1), gate_col + D,
                             why16, wo16, i, h, norm_ffn_g[i])
        r2, e2, cnt, e1 = _peer_route(hn, wq16, peer_k1, peer_k2, i)
        po = _peer_experts(hnt, peer_u, peer_v, r2, e2, cnt, e1, i)
        h = _ple(h1, po, norm_ple_g[i], wg16, p3, wp16, i, final_norm_g if i == depth - 1 else None)
    return h.reshape(B, L, D)
```

```python
import functools
import math

import jax
import jax.numpy as jnp
from jax import lax
from jax.experimental import pallas as pl
from jax.experimental.pallas import tpu as pltpu

F32 = jnp.float32
BF16 = jnp.bfloat16

NORM_EPS = 1e-6
POOL_WINDOWS = (2, 4, 8, 16)
SHORT_CONV = 3
FILTER_EMB = 33
DECAY_TARGET = 1e-2
DECAY_MIN = math.log(DECAY_TARGET) / 1.5
DECAY_MAX = math.log(DECAY_TARGET) / 0.3
PEER_TOPK = 16
LANES = 128
SUBLANES = 8
V7X_VMEM_BYTES = 64 * 1024 * 1024
VMEM_LIMIT = V7X_VMEM_BYTES - 8 * 1024 * 1024
HI = lax.Precision.HIGHEST


def _tile(n, pref):
    return pref if n % pref == 0 else n


def _params(sem, **options):
    return pltpu.CompilerParams(dimension_semantics=sem, vmem_limit_bytes=VMEM_LIMIT, **options)


def _bdot(a, b):
    return jnp.dot(a.astype(BF16), b.astype(BF16), preferred_element_type=F32)


def _rms(x, g):
    ms = jnp.mean(x * x, axis=-1, keepdims=True)
    return x * lax.rsqrt(ms + NORM_EPS) * g


def _inproj_kernel(x_ref, g_ref, w_ref, o_ref, xn_ref, *, n_plain):
    j = pl.program_id(1)

    @pl.when(j == 0)
    def _():
        xn_ref[...] = _rms(x_ref[...], g_ref[...]).astype(BF16)

    acc = jnp.dot(xn_ref[...], w_ref[0].astype(BF16), preferred_element_type=F32)
    o_ref[...] = jnp.where(j >= n_plain, jax.nn.sigmoid(acc), acc).astype(o_ref.dtype)


def _inproj(h, g, w, layer, n_gate_cols):
    T, D = h.shape
    N = w.shape[2]
    tm = _tile(T, 1024)
    tn = next(t for t in (1024, 512, 256, LANES) if N % t == 0 and (N - n_gate_cols) % t == 0)
    return pl.pallas_call(
        functools.partial(_inproj_kernel, n_plain=(N - n_gate_cols) // tn),
        grid=(T // tm, N // tn),
        in_specs=[pl.BlockSpec((tm, D), lambda i, j: (i, 0)),
                  pl.BlockSpec((1, D), lambda i, j: (0, 0)),
                  pl.BlockSpec((1, D, tn), lambda i, j: (layer, 0, j))],
        out_specs=pl.BlockSpec((tm, tn), lambda i, j: (i, j)),
        out_shape=jax.ShapeDtypeStruct((T, N), BF16),
        scratch_shapes=[pltpu.VMEM((tm, D), BF16)],
        compiler_params=_params(("parallel", "arbitrary")),
        name="inproj",
    )(h, g.reshape(1, D), w)


def _pool_kernel(pin_ref, gate_ref, pw_ref, ps_ref, wo_ref, o_ref, *, tm, kw, seq):
    t0 = pl.program_id(1) * tm
    k0 = pl.multiple_of(jnp.clip(t0 - (kw - tm) // 2, 0, seq - kw), LANES)
    r = t0 + lax.broadcasted_iota(jnp.int32, (tm, 1), 0)
    s = k0 + lax.broadcasted_iota(jnp.int32, (1, kw), 1)
    group = pw_ref.shape[2]
    parts = []
    for gi, win in enumerate(POOL_WINDOWS):
        lo = jnp.clip(r - win // 2, 0, seq - 1)
        hi = jnp.clip(r + win - win // 2 - 1, 0, seq - 1)
        inv = 1.0 / (hi - lo + 1).astype(F32)
        band = jnp.where((s >= lo) & (s <= hi), inv, 0.0) - jnp.where(s == r, 1.0, 0.0)
        u = pin_ref[0, pl.ds(k0, kw), gi * group:(gi + 1) * group]
        centred = jnp.dot(band.astype(BF16), u, preferred_element_type=F32)
        parts.append(_bdot(centred, pw_ref[0, gi]))
    pooled = jnp.concatenate(parts, axis=-1) * ps_ref[...]
    a = _bdot(pooled, wo_ref[0])
    o_ref[0] = (a * gate_ref[0].astype(F32)).astype(o_ref.dtype)


def _layer_spec(a, layer):
    nd = a.ndim - 1
    return pl.BlockSpec((1,) + a.shape[1:], lambda *_: (layer,) + (0,) * nd, pipeline_mode=pl.Buffered(1))


def _pool_branch(proj, pool_w16, pool_scale, w_pool_out16, layer, gate_col):
    B, L, _ = proj.shape
    _, pw, D = w_pool_out16.shape
    tm = _tile(L, 256)
    kw = min(L, 2 * tm)
    assert max(POOL_WINDOWS) <= (kw - tm) // 2 or kw == L
    return pl.pallas_call(
        functools.partial(_pool_kernel, tm=tm, kw=kw, seq=L),
        grid=(B, L // tm),
        in_specs=[pl.BlockSpec((1, L, pw), lambda b, i: (b, 0, 0)),
                  pl.BlockSpec((1, tm, D), lambda b, i: (b, i, gate_col // D)),
                  _layer_spec(pool_w16, layer),
                  pl.BlockSpec((1, pw), lambda b, i: (0, 0)),
                  _layer_spec(w_pool_out16, layer)],
        out_specs=pl.BlockSpec((1, tm, D), lambda b, i: (b, i, 0)),
        out_shape=jax.ShapeDtypeStruct((B, L, D), BF16),
        compiler_params=_params(("parallel", "arbitrary")),
        name="pool_branch",
    )(proj, proj, pool_w16, pool_scale.reshape(1, pw), w_pool_out16)


def _filter_kernel(fb_ref, w1t_ref, w1c_ref, w1s_ref, b1_ref, w2_ref, b2_ref, fr_ref, dl_ref,
                   w3a_ref, w3b_ref, w3c_ref, w3d_ref, hs_ref, hd_ref, nq_ref, hdn_ref, *, seq):
    @pl.when(pl.program_id(0) == 0)
    def _():
        ti = lax.broadcasted_iota(jnp.int32, (seq, 1), 0).astype(F32)
        t = ti / (seq - 1.0)
        ang = (2.0 * math.pi / seq) * ti * fb_ref[...]
        z = (t * w1t_ref[...] + jnp.dot(jnp.cos(ang), w1c_ref[...], precision=HI, preferred_element_type=F32)
             - jnp.dot(jnp.sin(ang), w1s_ref[...], precision=HI, preferred_element_type=F32))
        hdn = jnp.sin(fr_ref[...] * (z + b1_ref[...]))
        hdn = jnp.sin(fr_ref[...] * (jnp.dot(hdn, w2_ref[...], precision=HI, preferred_element_type=F32)
                                    + b2_ref[...]))
        hdn_ref[...] = hdn

    ti = lax.broadcasted_iota(jnp.int32, (seq, 1), 0)
    t = ti.astype(F32) / (seq - 1.0)
    decay = jnp.exp(-t * dl_ref[...])
    sign = jnp.where(ti % 2 == 0, 1.0, -1.0)

    def filt(w3_ref):
        f = jnp.dot(hdn_ref[...], w3_ref[...], precision=HI, preferred_element_type=F32) * decay
        return f / (jnp.sum(jnp.abs(f), axis=0, keepdims=True) + 1e-6)

    for order, (wf_ref, wb_ref) in enumerate(((w3a_ref, w3c_ref), (w3b_ref, w3d_ref))):
        hf, hb = filt(wf_ref), filt(wb_ref)
        hsum = hf + hb
        hs_ref[order] = hsum.astype(hs_ref.dtype)
        hd_ref[order] = (hf - hb).astype(hd_ref.dtype)
        nq_ref[order] = jnp.sum(hsum * sign, axis=0, keepdims=True)


def _pad_to(a, shape):
    return jnp.pad(a, [(0, s - d) for d, s in zip(a.shape, shape)])


def _hyena_filters(seq, w1, b1, w2, b2, w3, freq, width):
    hid = w1.shape[1]
    bands = (FILTER_EMB - 1) // 2
    hp = max(LANES, hid)
    fb = _pad_to(jnp.linspace(1e-4, bands - 1, bands, dtype=F32)[None, :], (1, LANES))
    w1t = _pad_to(w1[0:1], (1, hp))
    w1c = _pad_to(w1[1:1 + bands], (LANES, hp))
    w1s = _pad_to(w1[1 + bands:], (LANES, hp))
    b1p, b2p, frp = (_pad_to(v.reshape(1, hid), (1, hp)) for v in (b1, b2, freq))
    w2p = _pad_to(w2, (hp, hp))
    w3p = _pad_to(w3, (hp, w3.shape[1]))
    deltas = jnp.abs(jnp.linspace(DECAY_MIN, DECAY_MAX, width, dtype=F32))[None, :]
    tc = _tile(width, 256)
    nc = width // tc
    small = lambda shape: pl.BlockSpec(shape, lambda c: (0,) * len(shape))
    w3spec = lambda g: pl.BlockSpec((hp, tc), lambda c, g=g: (0, g * nc + c))
    out3 = lambda rows: pl.BlockSpec((2, rows, tc), lambda c: (0, 0, c))
    return pl.pallas_call(
        functools.partial(_filter_kernel, seq=seq),
        grid=(nc,),
        in_specs=[small((1, LANES)), small((1, hp)), small((LANES, hp)), small((LANES, hp)), small((1, hp)),
                  small((hp, hp)), small((1, hp)), small((1, hp)), pl.BlockSpec((1, tc), lambda c: (0, c)),
                  w3spec(0), w3spec(1), w3spec(2), w3spec(3)],
        out_specs=[out3(seq), out3(seq), out3(1)],
        out_shape=[jax.ShapeDtypeStruct((2, seq, width), BF16), jax.ShapeDtypeStruct((2, seq, width), BF16),
                   jax.ShapeDtypeStruct((2, 1, width), F32)],
        scratch_shapes=[pltpu.VMEM((seq, hp), F32)],
        compiler_params=_params(("arbitrary",)),
        name="hyena_filters",
    )(fb, w1t, w1c, w1s, b1p, w2p, b2p, frp, deltas, w3p, w3p, w3p, w3p)


def _dft_tables(seq, fc):
    n = 2 * seq
    t = jnp.arange(seq, dtype=jnp.int32)[None, :]
    assert n & (n - 1) == 0
    fine = 1 << (seq.bit_length() // 2)

    def trig(freqs):
        ang = ((freqs[:, None] * t) & (n - 1)).astype(F32) * (2.0 * math.pi / n)
        return jnp.cos(ang), jnp.sin(ang)

    (ch, sh), (cl, sl) = trig(fine * jnp.arange(seq // fine, dtype=jnp.int32)), trig(jnp.arange(fine, dtype=jnp.int32))
    cosm = (ch[:, None] * cl[None] - sh[:, None] * sl[None]).reshape(seq, seq)
    sinm = -(sh[:, None] * cl[None] + ch[:, None] * sl[None]).reshape(seq, seq)
    f = jnp.arange(seq, dtype=jnp.int32)[:, None]
    sinm = jnp.where(f == 0, jnp.where(t % 2 == 0, 1.0, -1.0), sinm)
    nk = seq // fc
    fwd = jnp.concatenate([cosm.reshape(nk, fc, seq), sinm.reshape(nk, fc, seq)], axis=1)
    inv = jnp.transpose(fwd, (0, 2, 1)) * (2.0 / n)
    return fwd.astype(BF16), inv.astype(BF16)


def _spectrum_kernel(w_ref, hs_ref, hd_ref, gr_ref, gi_ref, *, fc):
    gr_ref[0] = jnp.dot(w_ref[0, :fc, :], hs_ref[0], preferred_element_type=F32)
    gi_ref[0] = jnp.dot(w_ref[0, fc:, :], hd_ref[0], preferred_element_type=F32)


def _filter_spectrum(fwd, hs, hd):
    nk, fc2, seq = fwd.shape
    fc = fc2 // 2
    order, _, width = hs.shape
    tc = _tile(width, 512)
    hspec = pl.BlockSpec((1, seq, tc), lambda k, o, c: (o, 0, c))
    gspec = pl.BlockSpec((1, fc, tc), lambda k, o, c: (o, k, c))
    gshape = jax.ShapeDtypeStruct((order, seq, width), F32)
    return pl.pallas_call(
        functools.partial(_spectrum_kernel, fc=fc),
        grid=(nk, order, width // tc),
        in_specs=[pl.BlockSpec((1, fc2, seq), lambda k, o, c: (k, 0, 0)), hspec, hspec],
        out_specs=[gspec, gspec],
        out_shape=[gshape, gshape],
        compiler_params=_params(("arbitrary", "arbitrary", "arbitrary")),
        name="filter_spectrum",
    )(fwd, hs, hd)


def _hyena_kernel(v_ref, x1_ref, x2_ref, cwv_ref, cw1_ref, cw2_ref, cbv_ref, cb1_ref, cb2_ref,
                  fwd_ref, inv_ref, gr_ref, gi_ref, nq_ref, fbias_ref, o_ref,
                  u_ref, ub_ref, acc_ref, *, seq, fc):
    o, k = pl.program_id(2), pl.program_id(3)
    last = pl.num_programs(3) - 1
    row = lax.broadcasted_iota(jnp.int32, (seq, 1), 0)

    def short_conv(x_ref, w_ref, b_ref):
        x = x_ref[0].astype(F32)
        prev = jnp.where(row == 0, 0.0, pltpu.roll(x, 1, 0))
        nxt = jnp.where(row == seq - 1, 0.0, pltpu.roll(x, seq - 1, 0))
        return prev * w_ref[0:1, :] + x * w_ref[1:2, :] + nxt * w_ref[2:3, :] + b_ref[...]

    @pl.when((o == 0) & (k == 0))
    def _():
        v = short_conv(v_ref, cwv_ref, cbv_ref)
        u_ref[...] = v
        ub_ref[...] = v.astype(BF16)

    @pl.when(k == 0)
    def _():
        acc_ref[...] = jnp.zeros_like(acc_ref)

    spec = jnp.dot(fwd_ref[0], ub_ref[...], preferred_element_type=F32)
    vr, vi = spec[:fc], spec[fc:]
    gr, gi = gr_ref[0], gi_ref[0]
    yr, yi = vr * gr - vi * gi, vr * gi + vi * gr
    top = slice(0, SUBLANES)
    special = (lax.broadcasted_iota(jnp.int32, (SUBLANES, 1), 0) == 0) & (k == 0)
    yr0 = jnp.where(special, 0.5 * vr[top] * gr[top], yr[top])
    yi0 = jnp.where(special, 0.5 * vi[top] * nq_ref[0], yi[top])
    y = jnp.concatenate([yr0, yr[SUBLANES:], yi0, yi[SUBLANES:]], axis=0).astype(BF16)
    acc_ref[...] += jnp.dot(inv_ref[0], y, preferred_element_type=F32)

    @pl.when((k == last) & (o == 0))
    def _():
        z = (acc_ref[...] + u_ref[...] * fbias_ref[0]) * short_conv(x1_ref, cw1_ref, cb1_ref)
        u_ref[...] = z
        ub_ref[...] = z.astype(BF16)

    @pl.when((k == last) & (o == 1))
    def _():
        gate = short_conv(x2_ref, cw2_ref, cb2_ref)
        o_ref[0] = ((acc_ref[...] + u_ref[...] * fbias_ref[0]) * gate).astype(o_ref.dtype)


def _hyena_operator(proj, conv_w, conv_b, fwd, inv, gr, gi, nq, filt_bias, col0, width):
    B, L, _ = proj.shape
    nk, fc2, _ = fwd.shape
    tc = _tile(width, 512)
    nc = width // tc
    assert col0 % tc == 0
    xspec = lambda part: pl.BlockSpec((1, L, tc), lambda b, c, o, k, part=part: (b, 0, col0 // tc + part * nc + c),
                                      pipeline_mode=pl.Buffered(1))
    wspec = lambda rows, part: pl.BlockSpec((rows, tc), lambda b, c, o, k, part=part: (0, part * nc + c))
    ospec = lambda rows: pl.BlockSpec((1, rows, tc), lambda b, c, o, k: (o, 0, c))
    cb = conv_b.reshape(1, -1)
    return pl.pallas_call(
        functools.partial(_hyena_kernel, seq=L, fc=fc2 // 2),
        grid=(B, nc, 2, nk),
        in_specs=[xspec(0), xspec(1), xspec(2), wspec(SHORT_CONV, 0), wspec(SHORT_CONV, 1), wspec(SHORT_CONV, 2),
                  wspec(1, 0), wspec(1, 1), wspec(1, 2),
                  pl.BlockSpec((1, fc2, L), lambda b, c, o, k: (k, 0, 0)),
                  pl.BlockSpec((1, L, fc2), lambda b, c, o, k: (k, 0, 0)),
                  pl.BlockSpec((1, fc2 // 2, tc), lambda b, c, o, k: (o, k, c)),
                  pl.BlockSpec((1, fc2 // 2, tc), lambda b, c, o, k: (o, k, c)),
                  ospec(1), ospec(1)],
        out_specs=pl.BlockSpec((1, L, tc), lambda b, c, o, k: (b, 0, c)),
        out_shape=jax.ShapeDtypeStruct((B, L, width), BF16),
        scratch_shapes=[pltpu.VMEM((L, tc), F32), pltpu.VMEM((L, tc), BF16), pltpu.VMEM((L, tc), F32)],
        compiler_params=_params(("arbitrary",) * 4),
        name="hyena_operator",
    )(proj, proj, proj, conv_w, conv_w, conv_w, cb, cb, cb, fwd, inv, gr, gi, nq,
      filt_bias.reshape(2, 1, width))


def _merge_kernel(ga_ref, z_ref, gate_ref, wh_ref, wo_ref, h_ref, g_ref, h1_ref, hn_ref, hnt_ref):
    hy = jnp.dot(z_ref[...], wh_ref[0], preferred_element_type=F32)
    mix = ga_ref[...].astype(F32) + gate_ref[...].astype(F32) * hy
    h1 = h_ref[...] + _bdot(mix, wo_ref[0])
    h1_ref[...] = h1
    hn = _rms(h1, g_ref[...])
    hn_ref[...] = hn.astype(BF16)
    hnt_ref[...] = hn.T.astype(hnt_ref.dtype)


def _merge(ga, z2, proj, gate_col, whb, wob, layer, h, g):
    T, D = h.shape
    C = z2.shape[1]
    tm = _tile(T, 512)
    row = lambda w: pl.BlockSpec((tm, w), lambda i: (i, 0))
    full = lambda a: pl.BlockSpec(a.shape, lambda i: (0,) * a.ndim)
    g2 = g.reshape(1, D)
    return pl.pallas_call(
        _merge_kernel,
        grid=(T // tm,),
        in_specs=[row(D), row(C), pl.BlockSpec((tm, D), lambda i: (i, gate_col // D)),
                  _layer_spec(whb, layer), _layer_spec(wob, layer), row(D), full(g2)],
        out_specs=[row(D), row(D), pl.BlockSpec((D, tm), lambda i: (0, i))],
        out_shape=[jax.ShapeDtypeStruct((T, D), F32), jax.ShapeDtypeStruct((T, D), BF16),
                   jax.ShapeDtypeStruct((D, T), BF16)],
        compiler_params=_params(("parallel",)),
        name="mixer_merge",
    )(ga, z2, proj, whb, wob, h, g2)


PACKED_DTYPE = jnp.bfloat16
PACKED_ROWS = 2 * SUBLANES
WORD = jnp.uint32


def _sort_network(n):
    p = 1
    while p < n:
        p *= 2
    comps = []

    def merge(lo, cnt, r):
        step = 2 * r
        if step < cnt:
            merge(lo, cnt, step)
            merge(lo + r, cnt, step)
            comps.extend((i, i + r) for i in range(lo + r, lo + cnt - r, step))
        else:
            comps.append((lo, lo + r))

    def sort(lo, cnt):
        if cnt > 1:
            sort(lo, cnt // 2)
            sort(lo + cnt // 2, cnt // 2)
            merge(lo, cnt, 1)

    sort(0, p)
    return [(i, j) for i, j in comps if j < n]


def _top_sorted(x, count):
    lists = [x[SUBLANES * j:SUBLANES * (j + 1)] for j in range(x.shape[0] // SUBLANES)]
    for i, j in _sort_network(len(lists)):
        lists[i], lists[j] = jnp.maximum(lists[i], lists[j]), jnp.minimum(lists[i], lists[j])
    out = []
    for k in range(count):
        m = jnp.max(lists[0], axis=0, keepdims=True)
        out.append(m)
        hit = lists[0] == m
        for j in range(min(count - k - 1, len(lists))):
            below = lists[j + 1] if j + 1 < len(lists) else -jnp.inf
            lists[j] = jnp.where(hit, below, lists[j])
    return out


def _count_sorted(v, x, op):
    n = len(v)
    assert n & (n - 1) == 0
    count, taken, step = 0.0, [], n // 2
    while step >= 1:
        pivots = [v[base + step - 1] for base in range(0, n, 2 * step)]
        for t in taken:
            pivots = [jnp.where(t, hi, lo) for lo, hi in zip(pivots[0::2], pivots[1::2])]
        test = op(pivots[0], x)
        taken.insert(0, test)
        count = count + jnp.where(test, float(step), 0.0)
        step //= 2
    return count + jnp.where(op(v[n - 1], x), 1.0, 0.0)


def _route_kernel(hn_ref, wq_ref, k1_ref, k2_ref, r2_ref, e2_ref, cnt_ref, e1_ref, q_ref, *, heads, half):
    q_ref[...] = jnp.dot(hn_ref[...], wq_ref[0], preferred_element_type=F32)
    k1, k2 = k1_ref[0].astype(BF16), k2_ref[0].astype(BF16)
    nt = (((1,), (1,)), ((), ()))
    nbest = PEER_TOPK + 1
    padded = -(-nbest // SUBLANES) * SUBLANES
    assert nbest // 2 <= SUBLANES <= padded - SUBLANES
    tm = hn_ref.shape[0]

    def route_block(h, s1, s2, lanes):
        v1, v2 = _top_sorted(s1, nbest), _top_sorted(s2, nbest)
        pad = [jnp.full_like(v2[0], -jnp.inf)] * (padded - nbest)
        v1p, v2p = jnp.concatenate(v1 + pad, axis=0), jnp.concatenate(v2 + pad, axis=0)
        cand = jnp.concatenate([v1[0] + v2p, v1p[SUBLANES:] + v2[0]]
                               + [v1[a] + v2p[:SUBLANES] for a in range(1, SUBLANES)], axis=0)
        best = _top_sorted(cand, nbest)
        z = sum(jnp.exp(b - best[0]) for b in best[1:PEER_TOPK]) + 1.0
        thr = 0.5 * (best[PEER_TOPK - 1] + best[PEER_TOPK])
        need = thr - s1
        as_words = lambda x: pltpu.bitcast(x.astype(PACKED_DTYPE), WORD)
        r2_ref[h, :, lanes] = as_words(_count_sorted(v2[:PEER_TOPK], s2, jnp.greater))
        cnt_ref[h, :, lanes] = _count_sorted(v2[:PEER_TOPK], need, jnp.greater_equal)
        e2_ref[h, :, lanes] = as_words(jnp.exp(s2 - v2[0]) / z)
        e1_ref[h, :, lanes] = 0.5 * jnp.exp(s1 - v1[0])

    def route_head(h):
        base = pl.multiple_of(h * 2 * half, 2 * half)
        q1 = q_ref[:, pl.ds(base, half)].astype(BF16)
        q2 = q_ref[:, pl.ds(base + half, half)].astype(BF16)
        s1 = lax.dot_general(k1, q1, nt, preferred_element_type=F32)
        s2 = lax.dot_general(k2, q2, nt, preferred_element_type=F32)
        for tb in range(tm // LANES):
            lanes = slice(tb * LANES, (tb + 1) * LANES)
            route_block(h, s1[:, lanes], s2[:, lanes], lanes)

    group = 4 if heads % 4 == 0 else 1

    def body(g, carry):
        for r in range(group):
            route_head(g * group + r)
        return carry

    lax.fori_loop(0, heads // group, body, 0)


def _peer_route(hn, wqb, k1, k2, layer):
    T, D = hn.shape
    _, nk, half = k1.shape
    heads = wqb.shape[2] // (2 * half)
    tm = _tile(T, 512)
    ospec = lambda rows: pl.BlockSpec((heads, rows, tm), lambda i: (0, 0, i))
    wspec = lambda a: pl.BlockSpec((1,) + a.shape[1:], lambda i: (layer, 0, 0), pipeline_mode=pl.Buffered(1))
    return pl.pallas_call(
        functools.partial(_route_kernel, heads=heads, half=half),
        grid=(T // tm,),
        in_specs=[pl.BlockSpec((tm, D), lambda i: (i, 0)), wspec(wqb), wspec(k1), wspec(k2)],
        out_specs=[ospec(nk // 2), ospec(nk // 2), ospec(nk), ospec(nk)],
        out_shape=[jax.ShapeDtypeStruct((heads, nk // 2, T), WORD), jax.ShapeDtypeStruct((heads, nk // 2, T), WORD),
                   jax.ShapeDtypeStruct((heads, nk, T), F32), jax.ShapeDtypeStruct((heads, nk, T), F32)],
        scratch_shapes=[pltpu.VMEM((tm, wqb.shape[2]), F32)],
        compiler_params=_params(("parallel",)),
        name="peer_route",
    )(hn, wqb, k1, k2)


def _experts_kernel(hnt_ref, u_ref, v_ref, r2_ref, e2_ref, cnt_ref, e1_ref, o_ref, w_ref,
                    *, heads, nk, tm, te):
    j = pl.program_id(1)

    @pl.when(j == 0)
    def _():
        o_ref[...] = jnp.zeros_like(o_ref)

    groups = te // nk
    g0 = pl.multiple_of(j * groups, groups)
    na = 4

    nv = nk // PACKED_ROWS
    for tb in range(tm // LANES):
        lanes = slice(tb * LANES, (tb + 1) * LANES)
        for ag in range(groups // na):
            w = [[0.0] * nv for _ in range(na)]
            for h in range(heads):
                packed = lambda ref, c: pltpu.bitcast(ref[h, c * SUBLANES:(c + 1) * SUBLANES, lanes], PACKED_DTYPE)
                r2 = [packed(r2_ref, c) for c in range(nv)]
                e2 = [packed(e2_ref, c) for c in range(nv)]
                cg = cnt_ref[h, pl.ds(g0, groups), lanes]
                e1g = e1_ref[h, pl.ds(g0, groups), lanes]
                for a in range(na):
                    i = ag * na + a
                    cnt = jnp.broadcast_to(cg[i:i + 1], (PACKED_ROWS, LANES)).astype(PACKED_DTYPE)
                    e1 = jnp.broadcast_to(e1g[i:i + 1], (PACKED_ROWS, LANES)).astype(PACKED_DTYPE)
                    for c in range(nv):
                        w[a][c] = jnp.where(r2[c] < cnt, w[a][c] + e2[c] * e1, w[a][c]).astype(PACKED_DTYPE)
            for a in range(na):
                for c in range(nv):
                    r0 = ((ag * na + a) * nk) // 2 + c * SUBLANES
                    w_ref[r0:r0 + SUBLANES, lanes] = pltpu.bitcast(w[a][c], WORD)

    nn = (((1,), (0,)), ((), ()))
    hid = lax.dot_general(u_ref[0], hnt_ref[...], nn, preferred_element_type=F32)
    w = pltpu.bitcast(w_ref[...], PACKED_DTYPE)
    act = hid * (1.0 + lax.erf(hid * (1.0 / math.sqrt(2.0))))
    at = act.astype(PACKED_DTYPE) * w
    o_ref[...] += lax.dot_general(at.T, v_ref[0], nn, preferred_element_type=F32)


def _peer_experts(hnt, u_tab, v_tab, r2, e2, cnt, e1, layer):
    D, T = hnt.shape
    E = u_tab.shape[1]
    heads, nk, _ = cnt.shape
    tm, te = _tile(T, 512), _tile(E, 8 * nk)
    assert te == 8 * nk
    once = dict(pipeline_mode=pl.Buffered(1))
    rspec = lambda a: pl.BlockSpec(a.shape[:2] + (tm,), lambda i, j: (0, 0, i), **once)
    tspec = pl.BlockSpec((1, te, D), lambda i, j: (layer, j, 0))
    return pl.pallas_call(
        functools.partial(_experts_kernel, heads=heads, nk=nk, tm=tm, te=te),
        grid=(T // tm, E // te),
        in_specs=[pl.BlockSpec((D, tm), lambda i, j: (0, i), **once), tspec, tspec,
                  rspec(r2), rspec(e2), rspec(cnt), rspec(e1)],
        out_specs=pl.BlockSpec((tm, D), lambda i, j: (i, 0), **once),
        out_shape=jax.ShapeDtypeStruct((T, D), F32),
        scratch_shapes=[pltpu.VMEM((te // 2, tm), WORD)],
        compiler_params=_params(("parallel", "arbitrary")),
        name="peer_experts",
    )(hnt, u_tab, v_tab, r2, e2, cnt, e1)


def _ple_kernel(h_ref, d_ref, g_ref, wg_ref, p_ref, wp_ref, gf_ref, o_ref, *, final):
    h = h_ref[...] + d_ref[...]
    xn = _rms(h, g_ref[...]).astype(BF16)
    gate = jax.nn.sigmoid(jnp.dot(xn, wg_ref[0], preferred_element_type=F32))
    out = h + gate * _bdot(p_ref[0], wp_ref[0])
    o_ref[...] = _rms(out, gf_ref[...]) if final else out


def _ple(h, delta, g, w_gate16, p, w_proj16, layer, final_g=None):
    T, D = h.shape
    pd = p.shape[2]
    tm = _tile(T, 512)
    row = pl.BlockSpec((tm, D), lambda i: (i, 0))
    vec = pl.BlockSpec((1, D), lambda i: (0, 0))
    gf = (g if final_g is None else final_g).reshape(1, D)
    return pl.pallas_call(
        functools.partial(_ple_kernel, final=final_g is not None),
        grid=(T // tm,),
        in_specs=[row, row, vec, _layer_spec(w_gate16, layer),
                  pl.BlockSpec((1, tm, pd), lambda i: (layer, i, 0)), _layer_spec(w_proj16, layer), vec],
        out_specs=row,
        out_shape=jax.ShapeDtypeStruct((T, D), F32),
        compiler_params=_params(("parallel",)),
        name="ple",
    )(h, delta, g.reshape(1, D), w_gate16, p, w_proj16, gf)


def kernel(x, p, norm_mix_g, w_in, pool_w, pool_scale, w_pool_out, short_conv_w, short_conv_b, filt_w1, filt_b1, filt_w2, filt_b2, filt_w3, filt_freq, filt_bias, w_hyena_out, w_o, norm_ffn_g, peer_wq, peer_k1, peer_k2, peer_u, peer_v, norm_ple_g, ple_w_gate, ple_w_proj, final_norm_g):
    B, L, D = x.shape
    T = B * L
    depth = w_in.shape[0]
    pw = pool_scale.shape[-1]
    C = w_hyena_out.shape[1]
    hy_col, gate_col = pw, pw + 3 * C
    fwd, inv = _dft_tables(L, _tile(L, 512))
    pool_w16, wpo16, why16, wo16, wq16, wg16, wp16 = (
        w.astype(BF16) for w in (pool_w, w_pool_out, w_hyena_out, w_o, peer_wq, ple_w_gate, ple_w_proj))
    p3 = p.reshape(depth, T, -1)
    h = x.reshape(T, D)
    for i in range(depth):
        proj = _inproj(h, norm_mix_g[i], w_in, i, 2 * D).reshape(B, L, -1)
        ga = _pool_branch(proj, pool_w16, pool_scale[i], wpo16, i, gate_col)
        hs, hd, nq = _hyena_filters(L, filt_w1[i], filt_b1[i], filt_w2[i], filt_b2[i], filt_w3[i], filt_freq[i], C)
        gr, gi = _filter_spectrum(fwd, hs, hd)
        z2 = _hyena_operator(proj, short_conv_w[i], short_conv_b[i], fwd, inv, gr, gi, nq, filt_bias[i], hy_col, C)
        h1, hn, hnt = _merge(ga.reshape(T, D), z2.reshape(T, C), proj.reshape(T, -1), gate_col + D,
                             why16, wo16, i, h, norm_ffn_g[i])
        r2, e2, cnt, e1 = _peer_route(hn, wq16, peer_k1, peer_k2, i)
        po = _peer_experts(hnt, peer_u, peer_v, r2, e2, cnt, e1, i)
        h = _ple(h1, po, norm_ple_g[i], wg16, p3, wp16, i, final_norm_g if i == depth - 1 else None)
    return h.reshape(B, L, D)
```
